```python
import jax
import jax.numpy as jnp
from jax import lax
import numpy as np

D_MODEL = 4096
BATCH = 4
SEQ = 2048
DEPTH = 1
DEC_BATCH = 32
DEC_SEQ = 8
PAST_LEN = 8192
PAGE_SIZE = 128

N_HEADS = 16
HEAD_DIM = 128
ATTN_WIDTH = N_HEADS * HEAD_DIM
CONV_WIDTH = D_MODEL // 2
CONV_KERNEL = 31
CONV_STATE = CONV_KERNEL - 1
D_FF = 4 * D_MODEL
N_META = 16
Q_BLOCK = 128
EPS = 1e-6
SB_BIAS_INIT = -6.0
SPLITS = (ATTN_WIDTH, 2 * ATTN_WIDTH, 3 * ATTN_WIDTH,
          3 * ATTN_WIDTH + CONV_WIDTH, 3 * ATTN_WIDTH + 2 * CONV_WIDTH,
          3 * ATTN_WIDTH + 2 * CONV_WIDTH + D_MODEL)
W_IN_COLS = 3 * ATTN_WIDTH + 2 * CONV_WIDTH + 2 * D_MODEL

kernel_name = 'stickbreak_conformer_hybrid_step'


def rms_norm(x, g):
    xf = x.astype(jnp.float32)
    y = xf * lax.rsqrt(jnp.mean(xf * xf, axis=-1, keepdims=True) + EPS)
    return (y * g.astype(jnp.float32)).astype(x.dtype)


def layer_norm(x, g, b):
    xf = x.astype(jnp.float32)
    mu = jnp.mean(xf, axis=-1, keepdims=True)
    var = jnp.mean(jnp.square(xf - mu), axis=-1, keepdims=True)
    y = (xf - mu) * lax.rsqrt(var + EPS)
    return (y * g.astype(jnp.float32) + b.astype(jnp.float32)).astype(x.dtype)


def project(x, ln1_g, q_norm_g, k_norm_g, w_in):
    b, t, _ = x.shape
    xn = rms_norm(x, ln1_g)
    proj = jnp.einsum('btd,de->bte', xn, w_in)
    q, k, v, u_lin, u_gate, g_a, g_b = jnp.split(proj, SPLITS, axis=-1)
    q = rms_norm(q.reshape(b, t, N_HEADS, HEAD_DIM), q_norm_g)
    k = rms_norm(k.reshape(b, t, N_HEADS, HEAD_DIM), k_norm_g)
    v = v.reshape(b, t, N_HEADS, HEAD_DIM)
    u = u_lin * jax.nn.sigmoid(u_gate)
    return q, k, v, u, jax.nn.sigmoid(g_a), jax.nn.sigmoid(g_b)


def stick_breaking_block(q, k, v, sb_bias, q_pos, k_pos):
    z = jnp.einsum('bqhd,bkhd->bhqk', q, k, preferred_element_type=jnp.float32) * (HEAD_DIM ** -0.5)
    z = z + sb_bias.astype(jnp.float32)[None, :, None, None]
    visible = k_pos[None, :] < q_pos[:, None]
    log_keep = jnp.where(visible, jax.nn.log_sigmoid(-z), 0.0)
    after = lax.cumsum(log_keep, axis=3, reverse=True) - log_keep
    w = jnp.where(visible, jnp.exp(jax.nn.log_sigmoid(z) + after), 0.0)
    out = jnp.einsum('bhqk,bkhd->bqhd', w.astype(v.dtype), v, preferred_element_type=jnp.float32)
    return out.astype(v.dtype)


def stick_breaking_attention(q, k, v, sb_bias, past, bounds):
    pos = jnp.arange(k.shape[1])
    outs = [stick_breaking_block(q[:, a:e], k[:, :past + e], v[:, :past + e], sb_bias,
                                 pos[past + a:past + e], pos[:past + e]) for a, e in bounds]
    return jnp.concatenate(outs, axis=1)


def conv_branch(u, prefix, conv_w, conv_b, ln_g, ln_b):
    ext = jnp.concatenate([prefix.astype(u.dtype), u], axis=1)
    y = lax.conv_general_dilated(ext, conv_w[:, None, :].astype(u.dtype), (1,), 'VALID',
                                 dimension_numbers=('NWC', 'WIO', 'NWC'),
                                 feature_group_count=CONV_WIDTH)
    y = layer_norm(y + conv_b, ln_g, ln_b)
    return jax.nn.silu(y), ext[:, -CONV_STATE:]


def merge_and_mlp(x, attn, c, g_a, g_b, w_attn_out, w_conv_out, w_out, ln2_g, w_up, w_down):
    b, t, _ = x.shape
    a_d = jnp.einsum('bte,ed->btd', attn.reshape(b, t, ATTN_WIDTH), w_attn_out)
    c_d = jnp.einsum('btc,cd->btd', c, w_conv_out)
    h = x + jnp.einsum('bte,ed->btd', g_a * a_d + g_b * c_d, w_out)
    f = jnp.square(jax.nn.relu(jnp.einsum('btd,df->btf', rms_norm(h, ln2_g), w_up)))
    return h + jnp.einsum('btf,fd->btd', f, w_down)


def setup_inputs(seed: int = 0) -> dict:
    key = jax.random.key(seed)
    ks = jax.random.split(key, 24)
    n_pages = PAST_LEN // PAGE_SIZE
    n_used = DEC_BATCH * n_pages
    n_phys = n_used + (n_used + 3) // 4
    nrm = lambda k, shape, s: jax.random.normal(k, shape, jnp.float32) * s
    page_table = jax.random.permutation(ks[5], n_phys)[:n_used].reshape(DEC_BATCH, n_pages).astype(jnp.int32)
    return {
        'x_prompt': nrm(ks[0], (BATCH, SEQ, D_MODEL), 1.0),
        'x_sample': nrm(ks[1], (DEC_BATCH, DEC_SEQ, D_MODEL), 1.0),
        'cache_k': nrm(ks[2], (DEPTH, n_phys, PAGE_SIZE, N_HEADS, HEAD_DIM), 1.0),
        'cache_v': nrm(ks[3], (DEPTH, n_phys, PAGE_SIZE, N_HEADS, HEAD_DIM), 1.0),
        'state_conv': nrm(ks[4], (DEPTH, DEC_BATCH, CONV_STATE, CONV_WIDTH), 0.5),
        'page_table': page_table,
        'meta_tokens': nrm(ks[6], (N_META, D_MODEL), 1.0),
        'ln1_g': 1.0 + nrm(ks[7], (DEPTH, D_MODEL), 0.02),
        'q_norm_g': 1.0 + nrm(ks[8], (DEPTH, HEAD_DIM), 0.02),
        'k_norm_g': 1.0 + nrm(ks[9], (DEPTH, HEAD_DIM), 0.02),
        'sb_bias': SB_BIAS_INIT + nrm(ks[21], (DEPTH, N_HEADS), 0.1),
        'w_in': nrm(ks[10], (DEPTH, D_MODEL, W_IN_COLS), D_MODEL ** -0.5),
        'conv_w': nrm(ks[11], (DEPTH, CONV_KERNEL, CONV_WIDTH), CONV_KERNEL ** -0.5),
        'conv_b': nrm(ks[12], (DEPTH, CONV_WIDTH), 0.02),
        'conv_ln_g': 1.0 + nrm(ks[13], (DEPTH, CONV_WIDTH), 0.02),
        'conv_ln_b': nrm(ks[14], (DEPTH, CONV_WIDTH), 0.02),
        'w_attn_out': nrm(ks[15], (DEPTH, ATTN_WIDTH, D_MODEL), ATTN_WIDTH ** -0.5),
        'w_conv_out': nrm(ks[16], (DEPTH, CONV_WIDTH, D_MODEL), CONV_WIDTH ** -0.5),
        'w_out': nrm(ks[17], (DEPTH, D_MODEL, D_MODEL), D_MODEL ** -0.5),
        'ln2_g': 1.0 + nrm(ks[18], (DEPTH, D_MODEL), 0.02),
        'w_up': nrm(ks[19], (DEPTH, D_MODEL, D_FF), D_MODEL ** -0.5),
        'w_down': nrm(ks[20], (DEPTH, D_FF, D_MODEL), D_FF ** -0.5),
    }


def reference(x_prompt, x_sample, cache_k, cache_v, state_conv, page_table, meta_tokens,
              ln1_g, q_norm_g, k_norm_g, sb_bias, w_in, conv_w, conv_b, conv_ln_g, conv_ln_b,
              w_attn_out, w_conv_out, w_out, ln2_g, w_up, w_down):
    b_p = x_prompt.shape[0]
    b_s, t_s, _ = x_sample.shape
    past = page_table.shape[1] * cache_k.shape[2]
    meta = jnp.broadcast_to(meta_tokens.astype(x_prompt.dtype)[None], (b_p, N_META, D_MODEL))
    h_p = jnp.concatenate([meta, x_prompt], axis=1)
    h_s = x_sample
    l_p = h_p.shape[1]
    bounds_p = [(0, N_META)] + [(s, min(s + Q_BLOCK, l_p)) for s in range(N_META, l_p, Q_BLOCK)]
    bounds_s = [(a, min(a + Q_BLOCK, t_s)) for a in range(0, t_s, Q_BLOCK)]
    kp, vp, cp, ksm, vsm, csm = [], [], [], [], [], []
    for l in range(DEPTH):
        q, k, v, u, g_a, g_b = project(h_p, ln1_g[l], q_norm_g[l], k_norm_g[l], w_in[l])
        attn = stick_breaking_attention(q, k, v, sb_bias[l], 0, bounds_p)
        c, c_state = conv_branch(u, jnp.zeros((b_p, CONV_STATE, CONV_WIDTH), u.dtype),
                                 conv_w[l], conv_b[l], conv_ln_g[l], conv_ln_b[l])
        h_p = merge_and_mlp(h_p, attn, c, g_a, g_b, w_attn_out[l], w_conv_out[l], w_out[l],
                            ln2_g[l], w_up[l], w_down[l])
        kp.append(k)
        vp.append(v)
        cp.append(c_state)
        q, k, v, u, g_a, g_b = project(h_s, ln1_g[l], q_norm_g[l], k_norm_g[l], w_in[l])
        past_k = cache_k[l][page_table].reshape(b_s, past, N_HEADS, HEAD_DIM).astype(k.dtype)
        past_v = cache_v[l][page_table].reshape(b_s, past, N_HEADS, HEAD_DIM).astype(v.dtype)
        k_all = jnp.concatenate([past_k, k], axis=1)
        v_all = jnp.concatenate([past_v, v], axis=1)
        attn = stick_breaking_attention(q, k_all, v_all, sb_bias[l], past, bounds_s)
        c, c_state = conv_branch(u, state_conv[l], conv_w[l], conv_b[l], conv_ln_g[l], conv_ln_b[l])
        h_s = merge_and_mlp(h_s, attn, c, g_a, g_b, w_attn_out[l], w_conv_out[l], w_out[l],
                            ln2_g[l], w_up[l], w_down[l])
        ksm.append(k)
        vsm.append(v)
        csm.append(c_state)
    y_prompt = h_p[:, N_META:]
    y_sample = h_s
    return (y_prompt, y_sample, jnp.stack(kp), jnp.stack(vp), jnp.stack(cp),
            jnp.stack(ksm), jnp.stack(vsm), jnp.stack(csm))
```

```python
import functools

import jax
import jax.numpy as jnp
from jax import lax
from jax.experimental import pallas as pl
from jax.experimental.pallas import tpu as pltpu

N_HEADS = 16
HEAD_DIM = 128
N_META = 16
CONV_KERNEL = 31
CONV_STATE = CONV_KERNEL - 1
EPS = 1e-6
LANE = 128
VMEM_LIMIT = 56 * 1024 * 1024
F32 = jnp.float32
BF16 = jnp.bfloat16


def _cparams(sem):
    return pltpu.CompilerParams(dimension_semantics=sem, vmem_limit_bytes=VMEM_LIMIT)


def _norm_cast_kernel(x_ref, g_ref, o_ref):
    x = x_ref[...]
    ms = jnp.mean(x * x, axis=-1, keepdims=True)
    o_ref[...] = (x * lax.rsqrt(ms + EPS) * g_ref[...]).astype(o_ref.dtype)


def _norm_cast(x, g, tm):
    m, d = x.shape
    return pl.pallas_call(
        _norm_cast_kernel,
        grid=(m // tm,),
        in_specs=[pl.BlockSpec((tm, d), lambda i: (i, 0)),
                  pl.BlockSpec((1, d), lambda i: (0, 0))],
        out_specs=pl.BlockSpec((tm, d), lambda i: (i, 0)),
        out_shape=jax.ShapeDtypeStruct((m, d), BF16),
        compiler_params=_cparams(("parallel",)),
        name="norm_cast",
    )(x, g.reshape(1, d))


def _mm_kernel(*refs, nx, wx, nextra, epilogue):
    nw = len(wx)
    x_refs = refs[:nx]
    w_refs = refs[nx:nx + nw]
    e_refs = refs[nx + nw:nx + nw + nextra]
    o_refs = refs[nx + nw + nextra:]
    accs = [jnp.dot(x_refs[xi][...], w[...], preferred_element_type=F32) for xi, w in zip(wx, w_refs)]
    epilogue(accs, e_refs, o_refs)


def _mm(name, xs, ws, wx, w_col_off, extras, outs, epilogue, tm, tn):
    m = xs[0].shape[0]
    n_cols = outs[0][0]
    grid = (m // tm, n_cols // tn)
    in_specs = [pl.BlockSpec((tm, x.shape[1]), lambda i, j: (i, 0)) for x in xs]
    for w, off in zip(ws, w_col_off):
        blk = off // tn
        in_specs.append(pl.BlockSpec((w.shape[0], tn), lambda i, j, blk=blk: (0, j + blk)))
    in_specs += [pl.BlockSpec(bs, im) for _, bs, im in extras]
    return pl.pallas_call(
        functools.partial(_mm_kernel, nx=len(xs), wx=tuple(wx), nextra=len(extras), epilogue=epilogue),
        grid=grid,
        in_specs=in_specs,
        out_specs=[pl.BlockSpec((tm, tn), lambda i, j: (i, j)) for _ in outs],
        out_shape=[jax.ShapeDtypeStruct((m, nc), dt) for nc, dt in outs],
        compiler_params=_cparams(("parallel", "parallel")),
        name=name,
    )(*xs, *ws, *[a for a, _, _ in extras])


def _sigmoid(x):
    return 1.0 / (1.0 + jnp.exp(-x))


def _ep_headnorm(accs, e_refs, o_refs):
    (acc,), (g_ref,) = accs, e_refs
    g = g_ref[...]
    for hh in range(acc.shape[1] // HEAD_DIM):
        sl = slice(hh * HEAD_DIM, (hh + 1) * HEAD_DIM)
        blk = acc[:, sl]
        y = blk * lax.rsqrt(jnp.mean(blk * blk, axis=-1, keepdims=True) + EPS) * g
        for o in o_refs:
            o[:, sl] = y.astype(o.dtype)


def _ep_copy(accs, e_refs, o_refs):
    for o in o_refs:
        o[...] = accs[0].astype(o.dtype)


def _ep_glu(accs, e_refs, o_refs):
    o_refs[0][...] = accs[0] * _sigmoid(accs[1])


def _ep_sigmoid(accs, e_refs, o_refs):
    o_refs[0][...] = _sigmoid(accs[0]).astype(o_refs[0].dtype)


def _ep_merge(accs, e_refs, o_refs):
    ga = e_refs[0][...].astype(F32)
    gb = e_refs[1][...].astype(F32)
    o_refs[0][...] = (ga * accs[0] + gb * accs[1]).astype(o_refs[0].dtype)


def _ep_residual(accs, e_refs, o_refs):
    o_refs[0][...] = e_refs[0][...] + accs[0]


def _ep_relu2(accs, e_refs, o_refs):
    r = jnp.maximum(accs[0], 0.0)
    o_refs[0][...] = (r * r).astype(o_refs[0].dtype)


def _mm_k_kernel(x_ref, w_ref, r_ref, o_ref, acc_ref):
    k = pl.program_id(2)

    @pl.when(k == 0)
    def _():
        acc_ref[...] = jnp.zeros_like(acc_ref)

    acc_ref[...] += jnp.dot(x_ref[...], w_ref[...], preferred_element_type=F32)

    @pl.when(k == pl.num_programs(2) - 1)
    def _():
        o_ref[...] = r_ref[...] + acc_ref[...]


def _mm_k_residual(name, x, w, r, tm, tn, tk):
    m, kk = x.shape
    n = w.shape[1]
    return pl.pallas_call(
        _mm_k_kernel,
        grid=(m // tm, n // tn, kk // tk),
        in_specs=[pl.BlockSpec((tm, tk), lambda i, j, k: (i, k)),
                  pl.BlockSpec((tk, tn), lambda i, j, k: (k, j)),
                  pl.BlockSpec((tm, tn), lambda i, j, k: (i, j))],
        out_specs=pl.BlockSpec((tm, tn), lambda i, j, k: (i, j)),
        out_shape=jax.ShapeDtypeStruct((m, n), F32),
        scratch_shapes=[pltpu.VMEM((tm, tn), F32)],
        compiler_params=_cparams(("parallel", "parallel", "arbitrary")),
        name=name,
    )(x, w, r)


def _softplus(z):
    return jnp.maximum(z, 0.0) + jnp.log1p(jnp.exp(-jnp.abs(z)))


def _split_bf16(x):
    hi = x.astype(BF16)
    lo = (x - hi.astype(F32)).astype(BF16)
    return hi, lo


def _sb_tile_qrows(q, k, v, bias, vis, u2, acc, carry):
    tk = k.shape[0]
    z = lax.dot_general(q, k, (((1,), (1,)), ((), ())), preferred_element_type=F32)
    z = z * (HEAD_DIM ** -0.5) + bias
    sp = _softplus(z)
    log_keep = -sp
    if vis is not None:
        log_keep = jnp.where(vis, log_keep, 0.0)
    hi, lo = _split_bf16(log_keep)
    r = jnp.dot(jnp.concatenate([hi, lo], axis=0), u2, preferred_element_type=F32)
    tq = q.shape[0]
    r = r[:tq] + r[tq:]
    after = r[:, :tk] + carry
    w = jnp.exp(z - sp + after)
    if vis is not None:
        w = jnp.where(vis, w, 0.0)
    acc = acc + jnp.dot(w.astype(BF16), v, preferred_element_type=F32)
    return acc, carry + r[:, tk:]


def _u2_qrows(tk):
    j = lax.broadcasted_iota(jnp.int32, (tk, 2 * tk), 0)
    s = lax.broadcasted_iota(jnp.int32, (tk, 2 * tk), 1)
    return jnp.where((j > s) | (s >= tk), 1.0, 0.0).astype(BF16)


def _attn_prompt_kernel(bias_ref, q_ref, k_ref, v_ref, km_ref, vm_ref, o_ref, *, tq):
    h = pl.program_id(1)
    i = pl.program_id(2)
    bias = bias_ref[h]
    q = q_ref[...]
    u2 = _u2_qrows(tq)
    row = lax.broadcasted_iota(jnp.int32, (tq, tq), 0)
    col = lax.broadcasted_iota(jnp.int32, (tq, tq), 1)
    zero = jnp.zeros((tq, HEAD_DIM), F32)

    def kv(j):
        off = pl.multiple_of(j * tq, tq)
        return k_ref[pl.ds(off, tq), :], v_ref[pl.ds(off, tq), :]

    kd, vd = kv(i)
    acc, carry = _sb_tile_qrows(q, kd, vd, bias, col < row, u2, zero, zero)

    def body(s, c):
        kj, vj = kv(i - 1 - s)
        return _sb_tile_qrows(q, kj, vj, bias, None, u2, *c)

    acc, carry = lax.fori_loop(0, i, body, (acc, carry))
    acc, carry = _sb_tile_qrows(q, km_ref[...], vm_ref[...], bias, col < N_META, u2, acc, carry)
    o_ref[...] = acc.astype(o_ref.dtype)


def _attn_prompt(q, k, v, km, vm, sb_bias, batch, seq, tq):
    nq = seq // tq
    qo_spec = pl.BlockSpec((tq, HEAD_DIM), lambda b, h, i: (b * nq + i, h))
    kv_spec = pl.BlockSpec((seq, HEAD_DIM), lambda b, h, i: (b, h))
    m_spec = pl.BlockSpec((tq, HEAD_DIM), lambda b, h, i: (0, h))
    return pl.pallas_call(
        functools.partial(_attn_prompt_kernel, tq=tq),
        grid=(batch, N_HEADS, nq),
        in_specs=[pl.BlockSpec(memory_space=pltpu.SMEM), qo_spec, kv_spec, kv_spec, m_spec, m_spec],
        out_specs=qo_spec,
        out_shape=jax.ShapeDtypeStruct(q.shape, BF16),
        compiler_params=_cparams(("parallel", "parallel", "arbitrary")),
        name="attn_prompt",
    )(sb_bias, q, k, v, km, vm)


def _attn_meta_kernel(bias_ref, q_ref, k_ref, v_ref, o_ref, *, tq):
    bias = bias_ref[pl.program_id(0)]
    row = lax.broadcasted_iota(jnp.int32, (tq, tq), 0)
    col = lax.broadcasted_iota(jnp.int32, (tq, tq), 1)
    zero = jnp.zeros((tq, HEAD_DIM), F32)
    acc, _ = _sb_tile_qrows(q_ref[...], k_ref[...], v_ref[...], bias,
                            (col < row) & (col < N_META), _u2_qrows(tq), zero, zero)
    o_ref[...] = acc.astype(o_ref.dtype)


def _attn_meta(q, k, v, sb_bias, tq):
    spec = pl.BlockSpec((tq, HEAD_DIM), lambda h: (0, h))
    return pl.pallas_call(
        functools.partial(_attn_meta_kernel, tq=tq),
        grid=(N_HEADS,),
        in_specs=[pl.BlockSpec(memory_space=pltpu.SMEM), spec, spec, spec],
        out_specs=spec,
        out_shape=jax.ShapeDtypeStruct(q.shape, BF16),
        compiler_params=_cparams(("parallel",)),
        name="attn_meta",
    )(sb_bias, q, k, v)


def _sb_tile_krows(k, v, qbd, bias_cols, vis, ut, acc_ref, carry):
    z = jnp.dot(k, qbd, preferred_element_type=F32) * (HEAD_DIM ** -0.5) + bias_cols
    sp = _softplus(z)
    log_keep = -sp
    if vis is not None:
        log_keep = jnp.where(vis, log_keep, 0.0)
    hi, lo = _split_bf16(log_keep)
    r = jnp.dot(ut, jnp.concatenate([hi, lo], axis=1), preferred_element_type=F32)
    nc = z.shape[1]
    after = r[:, :nc] + r[:, nc:] + carry
    w = jnp.exp(z - sp + after)
    if vis is not None:
        w = jnp.where(vis, w, 0.0)
    acc_ref[...] += jnp.dot(w.T.astype(BF16), v, preferred_element_type=F32)
    return carry + jnp.sum(log_keep, axis=0, keepdims=True)


def _attn_sample_kernel(pt_ref, qbd_ref, bias_ref, kn_ref, vn_ref, *refs, pp, t_new):
    k_refs = refs[:pp]
    v_refs = refs[pp:2 * pp]
    o_ref, acc_ref, carry_ref = refs[2 * pp:]
    g = pl.program_id(1)
    page = kn_ref.shape[0]
    nc = qbd_ref.shape[1]
    qbd = qbd_ref[...]
    bias_cols = bias_ref[...]
    s = lax.broadcasted_iota(jnp.int32, (page, page), 0)
    j = lax.broadcasted_iota(jnp.int32, (page, page), 1)
    ut = jnp.where(j > s, 1.0, 0.0).astype(BF16)

    @pl.when(g == 0)
    def _():
        acc_ref[...] = jnp.zeros_like(acc_ref)
        key = lax.broadcasted_iota(jnp.int32, (page, nc), 0)
        t = lax.broadcasted_iota(jnp.int32, (page, nc), 1) % t_new
        carry_ref[...] = _sb_tile_krows(kn_ref[...], vn_ref[...], qbd, bias_cols, key < t, ut,
                                        acc_ref, jnp.zeros((1, nc), F32))

    carry = carry_ref[...]
    for p in range(pp):
        carry = _sb_tile_krows(k_refs[p][...].astype(BF16), v_refs[p][...].astype(BF16), qbd, bias_cols,
                               None, ut, acc_ref, carry)
    carry_ref[...] = carry

    @pl.when(g == pl.num_programs(1) - 1)
    def _():
        for hh in range(N_HEADS):
            sl = slice(hh * HEAD_DIM, (hh + 1) * HEAD_DIM)
            o_ref[:, sl] = acc_ref[hh * t_new:(hh + 1) * t_new, sl]


def _attn_sample(page_table, qbd, bias_cols, k_new, v_new, cache_k, cache_v, t_new, pp):
    nb, n_pages = page_table.shape
    page, width = cache_k.shape[1:]
    nc = qbd.shape[2]

    def page_spec(p):
        return pl.BlockSpec((None, page, width),
                            lambda b, g, pt, p=p: (pt[b, n_pages - 1 - (g * pp + p)], 0, 0))

    per_b = lambda b, g, pt: (b, 0, 0)
    grid_spec = pltpu.PrefetchScalarGridSpec(
        num_scalar_prefetch=1,
        grid=(nb, n_pages // pp),
        in_specs=[pl.BlockSpec((None, width, nc), per_b),
                  pl.BlockSpec((1, nc), lambda b, g, pt: (0, 0)),
                  pl.BlockSpec((None, page, width), per_b),
                  pl.BlockSpec((None, page, width), per_b)]
                 + [page_spec(p) for p in range(pp)] * 2,
        out_specs=pl.BlockSpec((None, t_new, width), per_b),
        scratch_shapes=[pltpu.VMEM((nc, width), F32), pltpu.VMEM((1, nc), F32)],
    )
    return pl.pallas_call(
        functools.partial(_attn_sample_kernel, pp=pp, t_new=t_new),
        grid_spec=grid_spec,
        out_shape=jax.ShapeDtypeStruct((nb, t_new, width), F32),
        compiler_params=_cparams(("parallel", "arbitrary")),
        name="attn_sample",
    )(page_table, qbd, bias_cols, k_new, v_new, *([cache_k] * pp), *([cache_v] * pp))


CONV_PAD = 32


def _conv_kernel(ext_ref, w_ref, b_ref, g_ref, beta_ref, o_ref, y_ref, *, seq, tt, lc):
    width = o_ref.shape[-1]
    lead = CONV_PAD - CONV_STATE

    def chunk(t0):
        for c in range(width // lc):
            cs = slice(c * lc, (c + 1) * lc)
            win = ext_ref[pl.ds(t0, tt + CONV_PAD), cs]
            acc = jnp.zeros((tt, lc), F32)
            for i in range(CONV_KERNEL):
                acc = acc + w_ref[i:i + 1, cs] * win[i + lead:i + lead + tt, :]
            y_ref[:, cs] = acc + b_ref[:, cs]
        y = y_ref[...]
        mu = jnp.mean(y, axis=-1, keepdims=True)
        d = y - mu
        var = jnp.mean(d * d, axis=-1, keepdims=True)
        yn = d * lax.rsqrt(var + EPS) * g_ref[...] + beta_ref[...]
        o_ref[pl.ds(t0, tt), :] = (yn * _sigmoid(yn)).astype(o_ref.dtype)

    if seq == tt:
        chunk(0)
    else:
        def body(n, _):
            chunk(pl.multiple_of(n * tt, tt))
            return 0
        lax.fori_loop(0, seq // tt, body, 0)


def _conv_branch(ext, conv_w, conv_b, ln_g, ln_b, tt):
    nseq, rows, width = ext.shape
    seq = rows - CONV_PAD
    vec = pl.BlockSpec((1, width), lambda n: (0, 0))
    return pl.pallas_call(
        functools.partial(_conv_kernel, seq=seq, tt=tt, lc=LANE),
        grid=(nseq,),
        in_specs=[pl.BlockSpec((None, rows, width), lambda n: (n, 0, 0)),
                  pl.BlockSpec((CONV_KERNEL, width), lambda n: (0, 0)), vec, vec, vec],
        out_specs=pl.BlockSpec((None, seq, width), lambda n: (n, 0, 0)),
        out_shape=jax.ShapeDtypeStruct((nseq, seq, width), BF16),
        scratch_shapes=[pltpu.VMEM((tt, width), F32)],
        compiler_params=_cparams(("parallel",)),
        name="conv_branch",
    )(ext, conv_w, conv_b.reshape(1, width), ln_g.reshape(1, width), ln_b.reshape(1, width))


def _project(x, ln1_g, q_g, k_g, w_in, tm, tn):
    d = x.shape[1]
    aw = N_HEADS * HEAD_DIM
    cw = d // 2
    xn = _norm_cast(x, ln1_g, min(tm, 256) if x.shape[0] % 256 == 0 else tm)
    gspec = lambda g: (g.reshape(1, HEAD_DIM), (1, HEAD_DIM), lambda i, j: (0, 0))
    (q,) = _mm("proj_q", [xn], [w_in], [0], [0], [gspec(q_g)], [(aw, BF16)], _ep_headnorm, tm, tn)
    k32, k16 = _mm("proj_k", [xn], [w_in], [0], [aw], [gspec(k_g)], [(aw, F32), (aw, BF16)], _ep_headnorm, tm, tn)
    v32, v16 = _mm("proj_v", [xn], [w_in], [0], [2 * aw], [], [(aw, F32), (aw, BF16)], _ep_copy, tm, tn)
    (u,) = _mm("proj_u", [xn], [w_in, w_in], [0, 0], [3 * aw, 3 * aw + cw], [], [(cw, F32)], _ep_glu, tm, tn // 2)
    (gates,) = _mm("proj_gates", [xn], [w_in], [0], [3 * aw + 2 * cw], [], [(2 * d, BF16)], _ep_sigmoid, tm, tn)
    return q, k32, k16, v32, v16, u, gates


def _merge_mlp(x, attn, c, gates, w_attn_out, w_conv_out, w_out, ln2_g, w_up, w_down, tm, tn):
    d = x.shape[1]
    nj = d // tn
    g_a = (gates, (tm, tn), lambda i, j: (i, j))
    g_b = (gates, (tm, tn), lambda i, j: (i, j + nj))
    (mrg,) = _mm("merge", [attn, c], [w_attn_out, w_conv_out], [0, 1], [0, 0], [g_a, g_b], [(d, BF16)],
                 _ep_merge, tm, tn)
    (h,) = _mm("out_proj", [mrg], [w_out], [0], [0], [(x, (tm, tn), lambda i, j: (i, j))], [(d, F32)],
               _ep_residual, tm, tn)
    hn = _norm_cast(h, ln2_g, min(tm, 256) if x.shape[0] % 256 == 0 else tm)
    (f,) = _mm("mlp_up", [hn], [w_up], [0], [0], [], [(w_up.shape[1], BF16)], _ep_relu2, tm, tn)
    return _mm_k_residual("mlp_down", f, w_down, h, tm, tn, 2048)


def kernel(x_prompt, x_sample, cache_k, cache_v, state_conv, page_table, meta_tokens, ln1_g, q_norm_g, k_norm_g,
           sb_bias, w_in, conv_w, conv_b, conv_ln_g, conv_ln_b, w_attn_out, w_conv_out, w_out, ln2_g, w_up, w_down):
    assert w_in.shape[0] == 1, "single-layer problem"
    batch, seq, d = x_prompt.shape
    nb, t_new, _ = x_sample.shape
    aw = N_HEADS * HEAD_DIM
    cw = d // 2
    page = cache_k.shape[2]
    w_in_b, w_ao_b, w_co_b, w_out_b, w_up_b, w_down_b = (
        w[0].astype(BF16) for w in (w_in, w_attn_out, w_conv_out, w_out, w_up, w_down))
    ln1, qg, kg, bias = ln1_g[0], q_norm_g[0], k_norm_g[0], sb_bias[0]
    cv_w, cv_b, cv_g, cv_beta = conv_w[0], conv_b[0], conv_ln_g[0], conv_ln_b[0]

    xp = x_prompt.reshape(batch * seq, d)
    xs = jnp.concatenate([meta_tokens, x_sample.reshape(nb * t_new, d)], axis=0)
    ms = xs.shape[0]
    tm_p, tn = 1024, 1024
    qp, kp32, kp16, vp32, vp16, up, gates_p = _project(xp, ln1, qg, kg, w_in_b, tm_p, tn)
    qs, ks32, ks16, vs32, vs16, us, gates_s = _project(xs, ln1, qg, kg, w_in_b, ms, tn)

    tq = 128
    pad_meta = lambda a: jnp.pad(a[:N_META], ((0, tq - N_META), (0, 0)))
    qm, km, vm = pad_meta(qs), pad_meta(ks16), pad_meta(vs16)
    attn_p = _attn_prompt(qp, kp16, vp16, km, vm, bias, batch, seq, tq)
    attn_m = _attn_meta(qm, km, vm, bias, tq)[:N_META]

    q4 = qs[N_META:].reshape(nb, t_new, N_HEADS, HEAD_DIM)
    qbd = jnp.einsum("bthd,hg->bhdgt", q4, jnp.eye(N_HEADS, dtype=BF16)).reshape(nb, aw, N_HEADS * t_new)
    bias_cols = jnp.repeat(bias.astype(F32), t_new).reshape(1, N_HEADS * t_new)
    pad_new = lambda a: jnp.pad(a[N_META:].reshape(nb, t_new, aw), ((0, 0), (0, page - t_new), (0, 0)))
    attn_s = _attn_sample(page_table, qbd, bias_cols, pad_new(ks16), pad_new(vs16),
                          cache_k[0].reshape(-1, page, aw), cache_v[0].reshape(-1, page, aw), t_new, 4)
    attn_small = jnp.concatenate([attn_m, attn_s.reshape(nb * t_new, aw).astype(BF16)], axis=0)

    lead = CONV_PAD - CONV_STATE
    u_meta = us[:N_META]
    u_s = us[N_META:].reshape(nb, t_new, cw)
    u_p = up.reshape(batch, seq, cw)
    hist_p = jnp.concatenate([jnp.zeros((CONV_PAD - N_META, cw), F32), u_meta], axis=0)
    ext_p = jnp.concatenate([jnp.broadcast_to(hist_p[None], (batch, CONV_PAD, cw)), u_p], axis=1)
    ext_m = jnp.concatenate([jnp.zeros((CONV_PAD, cw), F32), u_meta], axis=0)[None]
    ext_s = jnp.concatenate([jnp.zeros((nb, lead, cw), F32), state_conv[0], u_s], axis=1)
    c_p = _conv_branch(ext_p, cv_w, cv_b, cv_g, cv_beta, 64).reshape(batch * seq, cw)
    c_m = _conv_branch(ext_m, cv_w, cv_b, cv_g, cv_beta, N_META).reshape(N_META, cw)
    c_s = _conv_branch(ext_s, cv_w, cv_b, cv_g, cv_beta, t_new).reshape(nb * t_new, cw)
    c_small = jnp.concatenate([c_m, c_s], axis=0)

    y_p = _merge_mlp(xp, attn_p, c_p, gates_p, w_ao_b, w_co_b, w_out_b, ln2_g[0], w_up_b, w_down_b, tm_p, tn)
    y_s = _merge_mlp(xs, attn_small, c_small, gates_s, w_ao_b, w_co_b, w_out_b, ln2_g[0], w_up_b, w_down_b, ms, tn)

    def with_meta(real, small):
        meta = jnp.broadcast_to(small[:N_META].reshape(1, N_META, N_HEADS, HEAD_DIM),
                                (batch, N_META, N_HEADS, HEAD_DIM))
        return jnp.concatenate([meta, real.reshape(batch, seq, N_HEADS, HEAD_DIM)], axis=1)[None]

    heads = lambda a: a[N_META:].reshape(1, nb, t_new, N_HEADS, HEAD_DIM)
    return (y_p.reshape(batch, seq, d),
            y_s[N_META:].reshape(nb, t_new, d),
            with_meta(kp32, ks32),
            with_meta(vp32, vs32),
            ext_p[:, -CONV_STATE:][None],
            heads(ks32),
            heads(vs32),
            ext_s[:, -CONV_STATE:][None])
```

```python
import functools

import jax
import jax.numpy as jnp
from jax import lax
from jax.experimental import pallas as pl
from jax.experimental.pallas import tpu as pltpu

N_HEADS = 16
HEAD_DIM = 128
N_META = 16
CONV_KERNEL = 31
CONV_STATE = CONV_KERNEL - 1
EPS = 1e-6
LANE = 128
VMEM_LIMIT = 56 * 1024 * 1024
F32 = jnp.float32
BF16 = jnp.bfloat16


def _cparams(sem):
    return pltpu.CompilerParams(dimension_semantics=sem, vmem_limit_bytes=VMEM_LIMIT)


def _norm_cast_kernel(x_ref, g_ref, o_ref):
    x = x_ref[...]
    ms = jnp.mean(x * x, axis=-1, keepdims=True)
    o_ref[...] = (x * lax.rsqrt(ms + EPS) * g_ref[...]).astype(o_ref.dtype)


def _norm_cast(x, g, tm):
    m, d = x.shape
    return pl.pallas_call(
        _norm_cast_kernel,
        grid=(m // tm,),
        in_specs=[pl.BlockSpec((tm, d), lambda i: (i, 0)),
                  pl.BlockSpec((1, d), lambda i: (0, 0))],
        out_specs=pl.BlockSpec((tm, d), lambda i: (i, 0)),
        out_shape=jax.ShapeDtypeStruct((m, d), BF16),
        compiler_params=_cparams(("parallel",)),
        name="norm_cast",
    )(x, g.reshape(1, d))


def _mm_kernel(*refs, nx, wx, nextra, epilogue):
    nw = len(wx)
    x_refs = refs[:nx]
    w_refs = refs[nx:nx + nw]
    e_refs = refs[nx + nw:nx + nw + nextra]
    o_refs = refs[nx + nw + nextra:]
    accs = [jnp.dot(x_refs[xi][...], w[...], preferred_element_type=F32) for xi, w in zip(wx, w_refs)]
    epilogue(accs, e_refs, o_refs)


def _mm(name, xs, ws, wx, w_col_off, extras, outs, epilogue, tm, tn):
    m = xs[0].shape[0]
    n_cols = outs[0][0]
    grid = (m // tm, n_cols // tn)
    in_specs = [pl.BlockSpec((tm, x.shape[1]), lambda i, j: (i, 0)) for x in xs]
    for w, off in zip(ws, w_col_off):
        blk = off // tn
        in_specs.append(pl.BlockSpec((w.shape[0], tn), lambda i, j, blk=blk: (0, j + blk)))
    in_specs += [pl.BlockSpec(bs, im) for _, bs, im in extras]
    return pl.pallas_call(
        functools.partial(_mm_kernel, nx=len(xs), wx=tuple(wx), nextra=len(extras), epilogue=epilogue),
        grid=grid,
        in_specs=in_specs,
        out_specs=[pl.BlockSpec((tm, tn), lambda i, j: (i, j)) for _ in outs],
        out_shape=[jax.ShapeDtypeStruct((m, nc), dt) for nc, dt in outs],
        compiler_params=_cparams(("parallel", "parallel")),
        name=name,
    )(*xs, *ws, *[a for a, _, _ in extras])


def _sigmoid(x):
    return 1.0 / (1.0 + jnp.exp(-x))


def _ep_headnorm(accs, e_refs, o_refs):
    (acc,), (g_ref,) = accs, e_refs
    g = g_ref[...]
    for hh in range(acc.shape[1] // HEAD_DIM):
        sl = slice(hh * HEAD_DIM, (hh + 1) * HEAD_DIM)
        blk = acc[:, sl]
        y = blk * lax.rsqrt(jnp.mean(blk * blk, axis=-1, keepdims=True) + EPS) * g
        for o in o_refs:
            o[:, sl] = y.astype(o.dtype)


def _ep_copy(accs, e_refs, o_refs):
    for o in o_refs:
        o[...] = accs[0].astype(o.dtype)


def _ep_glu(accs, e_refs, o_refs):
    o_refs[0][...] = accs[0] * _sigmoid(accs[1])


def _ep_sigmoid(accs, e_refs, o_refs):
    o_refs[0][...] = _sigmoid(accs[0]).astype(o_refs[0].dtype)


def _ep_merge(accs, e_refs, o_refs):
    ga = e_refs[0][...].astype(F32)
    gb = e_refs[1][...].astype(F32)
    o_refs[0][...] = (ga * accs[0] + gb * accs[1]).astype(o_refs[0].dtype)


def _ep_residual(accs, e_refs, o_refs):
    o_refs[0][...] = e_refs[0][...] + accs[0]


def _ep_relu2(accs, e_refs, o_refs):
    r = jnp.maximum(accs[0], 0.0)
    o_refs[0][...] = (r * r).astype(o_refs[0].dtype)


def _mm_k_kernel(x_ref, w_ref, r_ref, o_ref, acc_ref):
    k = pl.program_id(2)

    @pl.when(k == 0)
    def _():
        acc_ref[...] = jnp.zeros_like(acc_ref)

    acc_ref[...] += jnp.dot(x_ref[...], w_ref[...], preferred_element_type=F32)

    @pl.when(k == pl.num_programs(2) - 1)
    def _():
        o_ref[...] = r_ref[...] + acc_ref[...]


def _mm_k_residual(name, x, w, r, tm, tn, tk):
    m, kk = x.shape
    n = w.shape[1]
    return pl.pallas_call(
        _mm_k_kernel,
        grid=(m // tm, n // tn, kk // tk),
        in_specs=[pl.BlockSpec((tm, tk), lambda i, j, k: (i, k)),
                  pl.BlockSpec((tk, tn), lambda i, j, k: (k, j)),
                  pl.BlockSpec((tm, tn), lambda i, j, k: (i, j))],
        out_specs=pl.BlockSpec((tm, tn), lambda i, j, k: (i, j)),
        out_shape=jax.ShapeDtypeStruct((m, n), F32),
        scratch_shapes=[pltpu.VMEM((tm, tn), F32)],
        compiler_params=_cparams(("parallel", "parallel", "arbitrary")),
        name=name,
    )(x, w, r)


def _split_bf16(x):
    hi = x.astype(BF16)
    lo = (x - hi.astype(F32)).astype(BF16)
    return hi, lo


def _neg_log_keep(z, vis):
    sp = jnp.maximum(z, 0.0) + jnp.log(1.0 + jnp.exp(-jnp.abs(z)))
    return sp if vis is None else jnp.where(vis, sp, 0.0)


def _strict_tri_neg(n, newer_on_rows):
    r = lax.broadcasted_iota(jnp.int32, (n, n), 0)
    c = lax.broadcasted_iota(jnp.int32, (n, n), 1)
    return jnp.where((r > c) if newer_on_rows else (c > r), -1.0, 0.0).astype(BF16)


def _sb_tile(q, k, v, bias, vis, u_neg, acc, carry):
    tq = q.shape[0]
    z = lax.dot_general(q, k, (((1,), (1,)), ((), ())), preferred_element_type=F32) + bias
    sp = _neg_log_keep(z, vis)
    hi, lo = _split_bf16(sp)
    r = jnp.dot(jnp.concatenate([hi, lo], axis=0), u_neg, preferred_element_type=F32)
    w = jnp.exp(z - sp + (r[:tq] + r[tq:]) + carry)
    if vis is not None:
        w = jnp.where(vis, w, 0.0)
    acc = acc + jnp.dot(w.astype(BF16), v, preferred_element_type=F32)
    return acc, carry - jnp.sum(sp, axis=1, keepdims=True)


def _attn_prompt_kernel(bias_ref, q_ref, k_ref, v_ref, km_ref, vm_ref, u_ref, o_ref, *, tq, tk, nh):
    hg = pl.program_id(1)
    i = pl.program_id(2)
    tmeta = km_ref.shape[0]
    jd = (i * tq) // tk
    row = lax.broadcasted_iota(jnp.int32, (tq, tk), 0)
    col = lax.broadcasted_iota(jnp.int32, (tq, tk), 1)
    vis_diag = jd * tk + col < i * tq + row
    vis_meta = lax.broadcasted_iota(jnp.int32, (tq, tmeta), 1) < N_META
    heads = [slice(hh * HEAD_DIM, (hh + 1) * HEAD_DIM) for hh in range(nh)]

    def chunk(j, state, vis):
        off = pl.multiple_of(j * tk, tk)
        return tuple(
            _sb_tile(q_ref[:, sl], k_ref[pl.ds(off, tk), sl], v_ref[pl.ds(off, tk), sl],
                     bias_ref[hg * nh + hh], vis, u_ref[...], *state[hh])
            for hh, sl in enumerate(heads))

    state = tuple((jnp.zeros((tq, HEAD_DIM), F32), jnp.zeros((tq, 1), F32)) for _ in heads)
    state = chunk(jd, state, vis_diag)
    state = lax.fori_loop(0, jd, lambda s, st: chunk(jd - 1 - s, st, None), state)
    for hh, sl in enumerate(heads):
        acc, _ = _sb_tile(q_ref[:, sl], km_ref[:, sl], vm_ref[:, sl], bias_ref[hg * nh + hh], vis_meta,
                          u_ref[:tmeta, :tmeta], *state[hh])
        o_ref[:, sl] = acc.astype(o_ref.dtype)


def _attn_prompt(q, k, v, km, vm, sb_bias, batch, seq, tq, tk, nh):
    nq = seq // tq
    wh = nh * HEAD_DIM
    qo_spec = pl.BlockSpec((tq, wh), lambda b, h, i: (b * nq + i, h))
    kv_spec = pl.BlockSpec((seq, wh), lambda b, h, i: (b, h))
    m_spec = pl.BlockSpec((km.shape[0], wh), lambda b, h, i: (0, h))
    return pl.pallas_call(
        functools.partial(_attn_prompt_kernel, tq=tq, tk=tk, nh=nh),
        grid=(batch, N_HEADS // nh, nq),
        in_specs=[pl.BlockSpec(memory_space=pltpu.SMEM), qo_spec, kv_spec, kv_spec, m_spec, m_spec,
                  pl.BlockSpec((tk, tk), lambda b, h, i: (0, 0))],
        out_specs=qo_spec,
        out_shape=jax.ShapeDtypeStruct(q.shape, BF16),
        compiler_params=_cparams(("parallel", "parallel", "arbitrary")),
        name="attn_prompt",
    )(sb_bias, q, k, v, km, vm, _strict_tri_neg(tk, True))


def _attn_meta_kernel(bias_ref, q_ref, k_ref, v_ref, o_ref):
    tq = q_ref.shape[0]
    row = lax.broadcasted_iota(jnp.int32, (tq, tq), 0)
    col = lax.broadcasted_iota(jnp.int32, (tq, tq), 1)
    acc, _ = _sb_tile(q_ref[...], k_ref[...], v_ref[...], bias_ref[pl.program_id(0)],
                      (col < row) & (col < N_META), _strict_tri_neg(tq, True),
                      jnp.zeros((tq, HEAD_DIM), F32), jnp.zeros((tq, 1), F32))
    o_ref[...] = acc.astype(o_ref.dtype)


def _attn_meta(q, k, v, sb_bias):
    spec = pl.BlockSpec((q.shape[0], HEAD_DIM), lambda h: (0, h))
    return pl.pallas_call(
        _attn_meta_kernel,
        grid=(N_HEADS,),
        in_specs=[pl.BlockSpec(memory_space=pltpu.SMEM), spec, spec, spec],
        out_specs=spec,
        out_shape=jax.ShapeDtypeStruct(q.shape, BF16),
        compiler_params=_cparams(("parallel",)),
        name="attn_meta",
    )(sb_bias, q, k, v)


def _sb_page_weights(z, vis, ut_neg, carry):
    nc = z.shape[1]
    sp = _neg_log_keep(z, vis)
    hi, lo = _split_bf16(sp)
    r = jnp.dot(ut_neg, jnp.concatenate([hi, lo], axis=1), preferred_element_type=F32)
    w = jnp.exp(z - sp + (r[:, :nc] + r[:, nc:]) + carry)
    if vis is not None:
        w = jnp.where(vis, w, 0.0)
    return w, carry - jnp.sum(sp, axis=0, keepdims=True)


def _attn_sample_kernel(pt_ref, qbd_ref, bias_ref, kn_ref, vn_ref, *refs, pp, t_new):
    k_refs = refs[:pp]
    v_refs = refs[pp:2 * pp]
    o_ref, acc_ref, carry_ref = refs[2 * pp:]
    g = pl.program_id(1)
    page = kn_ref.shape[0]
    nc = qbd_ref.shape[1]
    qbd = qbd_ref[...]
    bias_cols = bias_ref[...]
    ut_neg = _strict_tri_neg(page, False)

    def scores(k):
        return jnp.dot(k, qbd, preferred_element_type=F32) + bias_cols

    def accumulate(w, v):
        acc_ref[...] += jnp.dot(w.T.astype(BF16), v, preferred_element_type=F32)

    def heads_major(ref):
        return jnp.concatenate([ref[pl.ds(hh, page, stride=N_HEADS), :] for hh in range(N_HEADS)],
                               axis=1).astype(BF16)

    @pl.when(g == 0)
    def _():
        acc_ref[...] = jnp.zeros_like(acc_ref)
        key = lax.broadcasted_iota(jnp.int32, (page, nc), 0)
        t = lax.broadcasted_iota(jnp.int32, (page, nc), 1) % t_new
        w, carry = _sb_page_weights(scores(kn_ref[...]), key < t, ut_neg, jnp.zeros((1, nc), F32))
        accumulate(w, vn_ref[...])
        carry_ref[...] = carry

    z = scores(jnp.concatenate([heads_major(r) for r in k_refs], axis=0))
    carry = carry_ref[...]
    ws = []
    for p in range(pp):
        w, carry = _sb_page_weights(z[p * page:(p + 1) * page], None, ut_neg, carry)
        ws.append(w)
    carry_ref[...] = carry
    accumulate(jnp.concatenate(ws, axis=0), jnp.concatenate([heads_major(r) for r in v_refs], axis=0))

    @pl.when(g == pl.num_programs(1) - 1)
    def _():
        for hh in range(N_HEADS):
            sl = slice(hh * HEAD_DIM, (hh + 1) * HEAD_DIM)
            o_ref[:, sl] = acc_ref[hh * t_new:(hh + 1) * t_new, sl]


def _attn_sample(page_table, qbd, bias_cols, k_new, v_new, cache_k, cache_v, t_new, pp):
    nb, n_pages = page_table.shape
    page, width = k_new.shape[1:]
    nc = qbd.shape[2]

    def page_spec(p):
        return pl.BlockSpec((None,) + cache_k.shape[1:],
                            lambda b, g, pt, p=p: (pt[b, n_pages - 1 - (g * pp + p)], 0, 0))

    per_b = lambda b, g, pt: (b, 0, 0)
    grid_spec = pltpu.PrefetchScalarGridSpec(
        num_scalar_prefetch=1,
        grid=(nb, n_pages // pp),
        in_specs=[pl.BlockSpec((None, width, nc), per_b),
                  pl.BlockSpec((1, nc), lambda b, g, pt: (0, 0)),
                  pl.BlockSpec((None, page, width), per_b),
                  pl.BlockSpec((None, page, width), per_b)]
                 + [page_spec(p) for p in range(pp)] * 2,
        out_specs=pl.BlockSpec((None, t_new, width), per_b),
        scratch_shapes=[pltpu.VMEM((nc, width), F32), pltpu.VMEM((1, nc), F32)],
    )
    return pl.pallas_call(
        functools.partial(_attn_sample_kernel, pp=pp, t_new=t_new),
        grid_spec=grid_spec,
        out_shape=jax.ShapeDtypeStruct((nb, t_new, width), F32),
        compiler_params=_cparams(("parallel", "arbitrary")),
        name="attn_sample",
    )(page_table, qbd, bias_cols, k_new, v_new, *([cache_k] * pp), *([cache_v] * pp))


CONV_PAD = 32


def _conv_kernel(ext_ref, w_ref, b_ref, g_ref, beta_ref, o_ref, y_ref, *, seq, tt, lc):
    width = o_ref.shape[-1]
    lead = CONV_PAD - CONV_STATE

    def chunk(t0):
        for c in range(width // lc):
            cs = slice(c * lc, (c + 1) * lc)
            win = ext_ref[pl.ds(t0, tt + CONV_PAD), cs]
            acc = jnp.zeros((tt, lc), F32)
            for i in range(CONV_KERNEL):
                acc = acc + w_ref[i:i + 1, cs] * win[i + lead:i + lead + tt, :]
            y_ref[:, cs] = acc + b_ref[:, cs]
        y = y_ref[...]
        mu = jnp.mean(y, axis=-1, keepdims=True)
        d = y - mu
        var = jnp.mean(d * d, axis=-1, keepdims=True)
        yn = d * lax.rsqrt(var + EPS) * g_ref[...] + beta_ref[...]
        o_ref[pl.ds(t0, tt), :] = (yn * _sigmoid(yn)).astype(o_ref.dtype)

    if seq == tt:
        chunk(0)
    else:
        def body(n, _):
            chunk(pl.multiple_of(n * tt, tt))
            return 0
        lax.fori_loop(0, seq // tt, body, 0)


def _conv_branch(ext, conv_w, conv_b, ln_g, ln_b, tt):
    nseq, rows, width = ext.shape
    seq = rows - CONV_PAD
    vec = pl.BlockSpec((1, width), lambda n: (0, 0))
    return pl.pallas_call(
        functools.partial(_conv_kernel, seq=seq, tt=tt, lc=LANE),
        grid=(nseq,),
        in_specs=[pl.BlockSpec((None, rows, width), lambda n: (n, 0, 0)),
                  pl.BlockSpec((CONV_KERNEL, width), lambda n: (0, 0)), vec, vec, vec],
        out_specs=pl.BlockSpec((None, seq, width), lambda n: (n, 0, 0)),
        out_shape=jax.ShapeDtypeStruct((nseq, seq, width), BF16),
        scratch_shapes=[pltpu.VMEM((tt, width), F32)],
        compiler_params=_cparams(("parallel",)),
        name="conv_branch",
    )(ext, conv_w, conv_b.reshape(1, width), ln_g.reshape(1, width), ln_b.reshape(1, width))


def _project(x, ln1_g, q_g, k_g, w_in, tm, tn):
    d = x.shape[1]
    aw = N_HEADS * HEAD_DIM
    cw = d // 2
    xn = _norm_cast(x, ln1_g, min(tm, 256) if x.shape[0] % 256 == 0 else tm)
    gspec = lambda g: (g.reshape(1, HEAD_DIM), (1, HEAD_DIM), lambda i, j: (0, 0))
    (q,) = _mm("proj_q", [xn], [w_in], [0], [0], [gspec(q_g * HEAD_DIM ** -0.5)], [(aw, BF16)], _ep_headnorm, tm, tn)
    k32, k16 = _mm("proj_k", [xn], [w_in], [0], [aw], [gspec(k_g)], [(aw, F32), (aw, BF16)], _ep_headnorm, tm, tn)
    v32, v16 = _mm("proj_v", [xn], [w_in], [0], [2 * aw], [], [(aw, F32), (aw, BF16)], _ep_copy, tm, tn)
    (u,) = _mm("proj_u", [xn], [w_in, w_in], [0, 0], [3 * aw, 3 * aw + cw], [], [(cw, F32)], _ep_glu, tm, tn // 2)
    (gates,) = _mm("proj_gates", [xn], [w_in], [0], [3 * aw + 2 * cw], [], [(2 * d, BF16)], _ep_sigmoid, tm, tn)
    return q, k32, k16, v32, v16, u, gates


def _merge_mlp(x, attn, c, gates, w_attn_out, w_conv_out, w_out, ln2_g, w_up, w_down, tm, tn):
    d = x.shape[1]
    nj = d // tn
    g_a = (gates, (tm, tn), lambda i, j: (i, j))
    g_b = (gates, (tm, tn), lambda i, j: (i, j + nj))
    (mrg,) = _mm("merge", [attn, c], [w_attn_out, w_conv_out], [0, 1], [0, 0], [g_a, g_b], [(d, BF16)],
                 _ep_merge, tm, tn)
    (h,) = _mm("out_proj", [mrg], [w_out], [0], [0], [(x, (tm, tn), lambda i, j: (i, j))], [(d, F32)],
               _ep_residual, tm, tn)
    hn = _norm_cast(h, ln2_g, min(tm, 256) if x.shape[0] % 256 == 0 else tm)
    (f,) = _mm("mlp_up", [hn], [w_up], [0], [0], [], [(w_up.shape[1], BF16)], _ep_relu2, tm, tn)
    return _mm_k_residual("mlp_down", f, w_down, h, tm, tn, 2048)


def kernel(x_prompt, x_sample, cache_k, cache_v, state_conv, page_table, meta_tokens, ln1_g, q_norm_g, k_norm_g,
           sb_bias, w_in, conv_w, conv_b, conv_ln_g, conv_ln_b, w_attn_out, w_conv_out, w_out, ln2_g, w_up, w_down):
    assert w_in.shape[0] == 1, "single-layer problem"
    batch, seq, d = x_prompt.shape
    nb, t_new, _ = x_sample.shape
    aw = N_HEADS * HEAD_DIM
    cw = d // 2
    page = cache_k.shape[2]
    w_in_b, w_ao_b, w_co_b, w_out_b, w_up_b, w_down_b = (
        w[0].astype(BF16) for w in (w_in, w_attn_out, w_conv_out, w_out, w_up, w_down))
    ln1, qg, kg, bias = ln1_g[0], q_norm_g[0], k_norm_g[0], sb_bias[0]
    cv_w, cv_b, cv_g, cv_beta = conv_w[0], conv_b[0], conv_ln_g[0], conv_ln_b[0]

    xp = x_prompt.reshape(batch * seq, d)
    xs = jnp.concatenate([meta_tokens, x_sample.reshape(nb * t_new, d)], axis=0)
    ms = xs.shape[0]
    tm_p, tn = 1024, 1024
    qp, kp32, kp16, vp32, vp16, up, gates_p = _project(xp, ln1, qg, kg, w_in_b, tm_p, tn)
    qs, ks32, ks16, vs32, vs16, us, gates_s = _project(xs, ln1, qg, kg, w_in_b, ms, tn)

    pad_meta = lambda a: jnp.pad(a[:N_META], ((0, LANE - N_META), (0, 0)))
    qm, km, vm = pad_meta(qs), pad_meta(ks16), pad_meta(vs16)
    attn_p = _attn_prompt(qp, kp16, vp16, km, vm, bias, batch, seq, tq=128, tk=256, nh=4)
    attn_m = _attn_meta(qm, km, vm, bias)[:N_META]

    q4 = qs[N_META:].reshape(nb, t_new, N_HEADS, HEAD_DIM)
    qbd = jnp.einsum("bthd,hg->bhdgt", q4, jnp.eye(N_HEADS, dtype=BF16)).reshape(nb, aw, N_HEADS * t_new)
    bias_cols = jnp.repeat(bias.astype(F32), t_new).reshape(1, N_HEADS * t_new)
    pad_new = lambda a: jnp.pad(a[N_META:].reshape(nb, t_new, aw), ((0, 0), (0, page - t_new), (0, 0)))
    attn_s = _attn_sample(page_table, qbd, bias_cols, pad_new(ks16), pad_new(vs16),
                          cache_k[0].reshape(-1, page * N_HEADS, HEAD_DIM),
                          cache_v[0].reshape(-1, page * N_HEADS, HEAD_DIM), t_new, 4)
    attn_small = jnp.concatenate([attn_m, attn_s.reshape(nb * t_new, aw).astype(BF16)], axis=0)

    lead = CONV_PAD - CONV_STATE
    u_meta = us[:N_META]
    u_s = us[N_META:].reshape(nb, t_new, cw)
    u_p = up.reshape(batch, seq, cw)
    hist_p = jnp.concatenate([jnp.zeros((CONV_PAD - N_META, cw), F32), u_meta], axis=0)
    ext_p = jnp.concatenate([jnp.broadcast_to(hist_p[None], (batch, CONV_PAD, cw)), u_p], axis=1)
    ext_m = jnp.concatenate([jnp.zeros((CONV_PAD, cw), F32), u_meta], axis=0)[None]
    ext_s = jnp.concatenate([jnp.zeros((nb, lead, cw), F32), state_conv[0], u_s], axis=1)
    c_p = _conv_branch(ext_p, cv_w, cv_b, cv_g, cv_beta, 64).reshape(batch * seq, cw)
    c_m = _conv_branch(ext_m, cv_w, cv_b, cv_g, cv_beta, N_META).reshape(N_META, cw)
    c_s = _conv_branch(ext_s, cv_w, cv_b, cv_g, cv_beta, t_new).reshape(nb * t_new, cw)
    c_small = jnp.concatenate([c_m, c_s], axis=0)

    y_p = _merge_mlp(xp, attn_p, c_p, gates_p, w_ao_b, w_co_b, w_out_b, ln2_g[0], w_up_b, w_down_b, tm_p, tn)
    y_s = _merge_mlp(xs, attn_small, c_small, gates_s, w_ao_b, w_co_b, w_out_b, ln2_g[0], w_up_b, w_down_b, ms, tn)

    def with_meta(real, small):
        meta = jnp.broadcast_to(small[:N_META].reshape(1, N_META, N_HEADS, HEAD_DIM),
                                (batch, N_META, N_HEADS, HEAD_DIM))
        return jnp.concatenate([meta, real.reshape(batch, seq, N_HEADS, HEAD_DIM)], axis=1)[None]

    heads = lambda a: a[N_META:].reshape(1, nb, t_new, N_HEADS, HEAD_DIM)
    return (y_p.reshape(batch, seq, d),
            y_s[N_META:].reshape(nb, t_new, d),
            with_meta(kp32, ks32),
            with_meta(vp32, vs32),
            ext_p[:, -CONV_STATE:][None],
            heads(ks32),
            heads(vs32),
            ext_s[:, -CONV_STATE:][None])
```

```python
import functools

import jax
import jax.numpy as jnp
from jax import lax
from jax.experimental import pallas as pl
from jax.experimental.pallas import tpu as pltpu

N_HEADS = 16
HEAD_DIM = 128
N_META = 16
CONV_KERNEL = 31
CONV_STATE = CONV_KERNEL - 1
EPS = 1e-6
LANE = 128
VMEM_LIMIT = 56 * 1024 * 1024
F32 = jnp.float32
BF16 = jnp.bfloat16


def _cparams(sem):
    return pltpu.CompilerParams(dimension_semantics=sem, vmem_limit_bytes=VMEM_LIMIT)


def _norm_cast_kernel(x_ref, g_ref, o_ref):
    x = x_ref[...]
    ms = jnp.mean(x * x, axis=-1, keepdims=True)
    o_ref[...] = (x * lax.rsqrt(ms + EPS) * g_ref[...]).astype(o_ref.dtype)


def _norm_cast(x, g, tm):
    m, d = x.shape
    return pl.pallas_call(
        _norm_cast_kernel,
        grid=(m // tm,),
        in_specs=[pl.BlockSpec((tm, d), lambda i: (i, 0)),
                  pl.BlockSpec((1, d), lambda i: (0, 0))],
        out_specs=pl.BlockSpec((tm, d), lambda i: (i, 0)),
        out_shape=jax.ShapeDtypeStruct((m, d), BF16),
        compiler_params=_cparams(("parallel",)),
        name="norm_cast",
    )(x, g.reshape(1, d))


def _mm_kernel(*refs, nx, wx, nconst, nextra, nout, n_big, epilogue):
    nw = len(wx)
    pos = 0

    def take(n):
        nonlocal pos
        pos += n
        return refs[pos - n:pos]

    x_big, x_small, w_refs, consts = take(nx), take(nx), take(nw), take(nconst)
    e_big, e_small, o_big, o_small = take(nextra), take(nextra), take(nout), take(nout)
    i = pl.program_id(1)

    def run(x_refs, e_refs, o_refs):
        accs = [jnp.dot(x_refs[xi][...], w[...].astype(BF16), preferred_element_type=F32)
                for xi, w in zip(wx, w_refs)]
        epilogue(accs, consts + e_refs, o_refs)

    @pl.when(i < n_big)
    def _():
        run(x_big, e_big, o_big)

    @pl.when(i == n_big)
    def _():
        run(x_small, e_small, o_small)


def _mm(name, xs, ws, wx, w_col_off, consts, extras, outs, epilogue, tm, tn):
    mb, ms = xs[0][0].shape[0], xs[0][1].shape[0]
    n_big = mb // tm
    n_cols = outs[0][0]
    big_row = lambda i: jnp.minimum(i, n_big - 1)
    in_specs = [pl.BlockSpec((tm, xb.shape[1]), lambda j, i: (big_row(i), 0)) for xb, _ in xs]
    in_specs += [pl.BlockSpec((ms, xm.shape[1]), lambda j, i: (0, 0)) for _, xm in xs]
    in_specs += [pl.BlockSpec((w.shape[0], tn), lambda j, i, blk=off // tn: (0, j + blk)) for w, off in zip(ws, w_col_off)]
    in_specs += [pl.BlockSpec(bs, im) for _, bs, im in consts]
    in_specs += [pl.BlockSpec((tm, tn), lambda j, i, blk=off // tn: (big_row(i), j + blk)) for _, off in extras]
    in_specs += [pl.BlockSpec((ms, tn), lambda j, i, blk=off // tn: (0, j + blk)) for _, off in extras]
    out_specs = [pl.BlockSpec((tm, tn), lambda j, i: (big_row(i), j)) for _ in outs]
    out_specs += [pl.BlockSpec((ms, tn), lambda j, i: (0, j)) for _ in outs]
    res = pl.pallas_call(
        functools.partial(_mm_kernel, nx=len(xs), wx=tuple(wx), nconst=len(consts), nextra=len(extras),
                          nout=len(outs), n_big=n_big, epilogue=epilogue),
        grid=(n_cols // tn, n_big + 1),
        in_specs=in_specs,
        out_specs=out_specs,
        out_shape=[jax.ShapeDtypeStruct((mb, nc), dt) for nc, dt in outs]
                  + [jax.ShapeDtypeStruct((ms, nc), dt) for nc, dt in outs],
        compiler_params=_cparams(("parallel", "arbitrary")),
        name=name,
    )(*[xb for xb, _ in xs], *[xm for _, xm in xs], *ws, *[c for c, _, _ in consts],
      *[e[0] for e, _ in extras], *[e[1] for e, _ in extras])
    return [(res[k], res[len(outs) + k]) for k in range(len(outs))]


def _sigmoid(x):
    return 1.0 / (1.0 + jnp.exp(-x))


def _ep_headnorm(accs, e_refs, o_refs):
    (acc,), (g_ref,) = accs, e_refs
    g = g_ref[...]
    for hh in range(acc.shape[1] // HEAD_DIM):
        sl = slice(hh * HEAD_DIM, (hh + 1) * HEAD_DIM)
        blk = acc[:, sl]
        y = blk * lax.rsqrt(jnp.mean(blk * blk, axis=-1, keepdims=True) + EPS) * g
        for o in o_refs:
            o[:, sl] = y.astype(o.dtype)


def _ep_copy(accs, e_refs, o_refs):
    for o in o_refs:
        o[...] = accs[0].astype(o.dtype)


def _ep_glu(accs, e_refs, o_refs):
    o_refs[0][...] = accs[0] * _sigmoid(accs[1])


def _ep_sigmoid(accs, e_refs, o_refs):
    o_refs[0][...] = _sigmoid(accs[0]).astype(o_refs[0].dtype)


def _ep_merge(accs, e_refs, o_refs):
    ga = e_refs[0][...].astype(F32)
    gb = e_refs[1][...].astype(F32)
    o_refs[0][...] = (ga * accs[0] + gb * accs[1]).astype(o_refs[0].dtype)


def _ep_residual(accs, e_refs, o_refs):
    o_refs[0][...] = e_refs[0][...] + accs[0]


def _ep_relu2(accs, e_refs, o_refs):
    r = jnp.maximum(accs[0], 0.0)
    o_refs[0][...] = (r * r).astype(o_refs[0].dtype)


def _mm_k_kernel(x_ref, w_ref, r_ref, o_ref, acc_ref):
    k = pl.program_id(2)

    @pl.when(k == 0)
    def _():
        acc_ref[...] = jnp.zeros_like(acc_ref)

    acc_ref[...] += jnp.dot(x_ref[...], w_ref[...], preferred_element_type=F32)

    @pl.when(k == pl.num_programs(2) - 1)
    def _():
        o_ref[...] = r_ref[...] + acc_ref[...]


def _mm_k_residual(name, x, w, r, tm, tn, tk):
    m, kk = x.shape
    n = w.shape[1]
    return pl.pallas_call(
        _mm_k_kernel,
        grid=(m // tm, n // tn, kk // tk),
        in_specs=[pl.BlockSpec((tm, tk), lambda i, j, k: (i, k)),
                  pl.BlockSpec((tk, tn), lambda i, j, k: (k, j)),
                  pl.BlockSpec((tm, tn), lambda i, j, k: (i, j))],
        out_specs=pl.BlockSpec((tm, tn), lambda i, j, k: (i, j)),
        out_shape=jax.ShapeDtypeStruct((m, n), F32),
        scratch_shapes=[pltpu.VMEM((tm, tn), F32)],
        compiler_params=_cparams(("parallel", "parallel", "arbitrary")),
        name=name,
    )(x, w, r)


def _split_bf16(x):
    hi = x.astype(BF16)
    lo = (x - hi.astype(F32)).astype(BF16)
    return hi, lo


def _neg_log_keep(z, vis):
    sp = jnp.maximum(z, 0.0) + jnp.log(1.0 + jnp.exp(-jnp.abs(z)))
    return sp if vis is None else jnp.where(vis, sp, 0.0)


def _strict_tri_neg(n, newer_on_rows):
    r = lax.broadcasted_iota(jnp.int32, (n, n), 0)
    c = lax.broadcasted_iota(jnp.int32, (n, n), 1)
    return jnp.where((r > c) if newer_on_rows else (c > r), -1.0, 0.0).astype(BF16)


def _sb_tile(q, k, v, bias, vis, u_neg, acc, carry):
    tq = q.shape[0]
    z = lax.dot_general(q, k, (((1,), (1,)), ((), ())), preferred_element_type=F32) + bias
    sp = _neg_log_keep(z, vis)
    hi, lo = _split_bf16(sp)
    r = jnp.dot(jnp.concatenate([hi, lo], axis=0), u_neg, preferred_element_type=F32)
    w = jnp.exp(z - sp + (r[:tq] + r[tq:]) + carry)
    if vis is not None:
        w = jnp.where(vis, w, 0.0)
    acc = acc + jnp.dot(w.astype(BF16), v, preferred_element_type=F32)
    return acc, carry - jnp.sum(sp, axis=1, keepdims=True)


def _attn_prompt_kernel(bias_ref, q_ref, k_ref, v_ref, km_ref, vm_ref, u_ref, o_ref,
                        sp_ref, ls_ref, w_ref, acc_ref, carry_ref, tot_ref, *, tq, tk, nh):
    hg = pl.program_id(1)
    i = pl.program_id(2)
    tmeta = km_ref.shape[0]
    row = lax.broadcasted_iota(jnp.int32, (tq, tk), 0)
    col = lax.broadcasted_iota(jnp.int32, (tq, tk), 1)
    vis_diag = col < row
    vis_meta = lax.broadcasted_iota(jnp.int32, (tq, tmeta), 1) < N_META
    heads = [slice(hh * HEAD_DIM, (hh + 1) * HEAD_DIM) for hh in range(nh)]
    acc_ref[...] = jnp.zeros_like(acc_ref)
    carry_ref[...] = jnp.zeros_like(carry_ref)

    def tile_group(load_k, load_v, vis, u, width):
        cols = slice(0, width)
        for hh, sl in enumerate(heads):
            z = lax.dot_general(q_ref[:, sl], load_k(sl), (((1,), (1,)), ((), ())),
                                preferred_element_type=F32) + bias_ref[hg * nh + hh]
            sp = _neg_log_keep(z, vis)
            hi, lo = _split_bf16(sp)
            sp_ref[hh, :tq, cols] = hi
            sp_ref[hh, tq:, cols] = lo
            ls_ref[hh, :, cols] = z - sp
            tot_ref[hh] = jnp.broadcast_to(jnp.sum(sp, axis=1, keepdims=True), (tq, HEAD_DIM))
        for hh, sl in enumerate(heads):
            r = jnp.dot(sp_ref[hh, :, cols], u, preferred_element_type=F32)
            c = carry_ref[hh]
            c = c if width == HEAD_DIM else jnp.concatenate([c] * (width // HEAD_DIM), axis=1)
            w = jnp.exp(ls_ref[hh, :, cols] + (r[:tq] + r[tq:]) + c)
            if vis is not None:
                w = jnp.where(vis, w, 0.0)
            w_ref[hh, :, cols] = w.astype(BF16)
        for hh, sl in enumerate(heads):
            acc_ref[hh] += jnp.dot(w_ref[hh, :, cols], load_v(sl), preferred_element_type=F32)
            carry_ref[hh] -= tot_ref[hh]

    def chunk(j, vis):
        off = pl.multiple_of(j * tk, tk)
        tile_group(lambda sl: k_ref[pl.ds(off, tk), sl], lambda sl: v_ref[pl.ds(off, tk), sl], vis, u_ref[...], tk)

    chunk(i, vis_diag)

    def body(s, _):
        chunk(i - 1 - s, None)
        return 0

    lax.fori_loop(0, i, body, 0)
    tile_group(lambda sl: km_ref[:, sl], lambda sl: vm_ref[:, sl], vis_meta, u_ref[:tmeta, :tmeta], tmeta)
    for hh, sl in enumerate(heads):
        o_ref[:, sl] = acc_ref[hh].astype(o_ref.dtype)


def _attn_prompt(q, k, v, km, vm, sb_bias, batch, seq, tq, nh):
    tk = tq
    nq = seq // tq
    wh = nh * HEAD_DIM
    qo_spec = pl.BlockSpec((tq, wh), lambda b, h, i: (b * nq + i, h))
    kv_spec = pl.BlockSpec((seq, wh), lambda b, h, i: (b, h))
    m_spec = pl.BlockSpec((km.shape[0], wh), lambda b, h, i: (0, h))
    return pl.pallas_call(
        functools.partial(_attn_prompt_kernel, tq=tq, tk=tk, nh=nh),
        grid=(batch, N_HEADS // nh, nq),
        in_specs=[pl.BlockSpec(memory_space=pltpu.SMEM), qo_spec, kv_spec, kv_spec, m_spec, m_spec,
                  pl.BlockSpec((tk, tk), lambda b, h, i: (0, 0))],
        out_specs=qo_spec,
        out_shape=jax.ShapeDtypeStruct(q.shape, BF16),
        scratch_shapes=[pltpu.VMEM((nh, 2 * tq, tk), BF16), pltpu.VMEM((nh, tq, tk), F32),
                        pltpu.VMEM((nh, tq, tk), BF16), pltpu.VMEM((nh, tq, HEAD_DIM), F32),
                        pltpu.VMEM((nh, tq, HEAD_DIM), F32), pltpu.VMEM((nh, tq, HEAD_DIM), F32)],
        compiler_params=_cparams(("parallel", "parallel", "arbitrary")),
        name="attn_prompt",
    )(sb_bias, q, k, v, km, vm, _strict_tri_neg(tk, True))


def _attn_meta_kernel(bias_ref, q_ref, k_ref, v_ref, o_ref):
    tq = q_ref.shape[0]
    row = lax.broadcasted_iota(jnp.int32, (tq, tq), 0)
    col = lax.broadcasted_iota(jnp.int32, (tq, tq), 1)
    acc, _ = _sb_tile(q_ref[...], k_ref[...], v_ref[...], bias_ref[pl.program_id(0)],
                      (col < row) & (col < N_META), _strict_tri_neg(tq, True),
                      jnp.zeros((tq, HEAD_DIM), F32), jnp.zeros((tq, 1), F32))
    o_ref[...] = acc.astype(o_ref.dtype)


def _attn_meta(q, k, v, sb_bias):
    spec = pl.BlockSpec((q.shape[0], HEAD_DIM), lambda h: (0, h))
    return pl.pallas_call(
        _attn_meta_kernel,
        grid=(N_HEADS,),
        in_specs=[pl.BlockSpec(memory_space=pltpu.SMEM), spec, spec, spec],
        out_specs=spec,
        out_shape=jax.ShapeDtypeStruct(q.shape, BF16),
        compiler_params=_cparams(("parallel",)),
        name="attn_meta",
    )(sb_bias, q, k, v)


def _sb_page_weights(z, vis, ut_neg, carry):
    nc = z.shape[1]
    sp = _neg_log_keep(z, vis)
    hi, lo = _split_bf16(sp)
    r = jnp.dot(ut_neg, jnp.concatenate([hi, lo], axis=1), preferred_element_type=F32)
    w = jnp.exp(z - sp + (r[:, :nc] + r[:, nc:]) + carry)
    if vis is not None:
        w = jnp.where(vis, w, 0.0)
    return w, carry - jnp.sum(sp, axis=0, keepdims=True)


HEAD_PITCH = 24


def _attn_sample_kernel(pt_ref, qbd_ref, bias_ref, kn_ref, vn_ref, ck_ref, cv_ref, o_ref,
                        kbuf, vbuf, sem, acc_ref, *, pp, t_new, n_pages):
    b = pl.program_id(0)
    page = kn_ref.shape[0]
    nc = qbd_ref.shape[1]
    ngroups = n_pages // pp
    qbd = qbd_ref[...]
    bias_cols = bias_ref[...]
    ut_neg = _strict_tri_neg(page, False)

    def copies(bb, g, slot):
        out = []
        for p in range(pp):
            idx = pt_ref[bb, n_pages - 1 - (g * pp + p)]
            for t, (src, dst) in enumerate(((ck_ref, kbuf), (cv_ref, vbuf))):
                out.append(pltpu.make_async_copy(src.at[idx], dst.at[slot, p, :, pl.ds(0, N_HEADS), :],
                                                 sem.at[t, slot, p]))
        return out

    def scores(k):
        return jnp.dot(k, qbd, preferred_element_type=F32) + bias_cols

    def accumulate(w, v):
        acc_ref[...] += jnp.dot(w.T.astype(BF16), v, preferred_element_type=F32)

    def heads_major(buf, slot):
        return jnp.concatenate(
            [jnp.concatenate([buf.at[slot, p].reshape(page * HEAD_PITCH, HEAD_DIM)[pl.ds(hh, page, stride=HEAD_PITCH), :]
                              for hh in range(N_HEADS)], axis=1) for p in range(pp)],
            axis=0).astype(BF16)

    @pl.when(b == 0)
    def _():
        for c in copies(0, 0, 0):
            c.start()

    acc_ref[...] = jnp.zeros_like(acc_ref)
    key = lax.broadcasted_iota(jnp.int32, (page, nc), 0)
    t = lax.broadcasted_iota(jnp.int32, (page, nc), 1) % t_new
    w, carry = _sb_page_weights(scores(kn_ref[...]), key < t, ut_neg, jnp.zeros((1, nc), F32))
    accumulate(w, vn_ref[...])

    def body(g, carry):
        slot = g % 2

        @pl.when(g + 1 < ngroups)
        def _():
            for c in copies(b, g + 1, 1 - slot):
                c.start()

        @pl.when((g + 1 == ngroups) & (b + 1 < pl.num_programs(0)))
        def _():
            for c in copies(b + 1, 0, 0):
                c.start()

        for c in copies(b, g, slot):
            c.wait()
        z = scores(heads_major(kbuf, slot))
        ws = []
        for p in range(pp):
            w, carry = _sb_page_weights(z[p * page:(p + 1) * page], None, ut_neg, carry)
            ws.append(w)
        accumulate(jnp.concatenate(ws, axis=0), heads_major(vbuf, slot))
        return carry

    lax.fori_loop(0, ngroups, body, carry)
    for hh in range(N_HEADS):
        sl = slice(hh * HEAD_DIM, (hh + 1) * HEAD_DIM)
        o_ref[:, sl] = acc_ref[hh * t_new:(hh + 1) * t_new, sl]


def _attn_sample(page_table, qbd, bias_cols, k_new, v_new, cache_k, cache_v, t_new, pp):
    nb, n_pages = page_table.shape
    assert (n_pages // pp) % 2 == 0, "page groups alternate between two buffer slots, starting each sequence in slot 0"
    page, width = k_new.shape[1:]
    nc = qbd.shape[2]
    per_b = lambda b, pt: (b, 0, 0)
    grid_spec = pltpu.PrefetchScalarGridSpec(
        num_scalar_prefetch=1,
        grid=(nb,),
        in_specs=[pl.BlockSpec((None, width, nc), per_b),
                  pl.BlockSpec((1, nc), lambda b, pt: (0, 0)),
                  pl.BlockSpec((None, page, width), per_b),
                  pl.BlockSpec((None, page, width), per_b),
                  pl.BlockSpec(memory_space=pl.ANY),
                  pl.BlockSpec(memory_space=pl.ANY)],
        out_specs=pl.BlockSpec((None, t_new, width), per_b),
        scratch_shapes=[pltpu.VMEM((2, pp, page, HEAD_PITCH, HEAD_DIM), F32),
                        pltpu.VMEM((2, pp, page, HEAD_PITCH, HEAD_DIM), F32),
                        pltpu.SemaphoreType.DMA((2, 2, pp)),
                        pltpu.VMEM((nc, width), F32)],
    )
    return pl.pallas_call(
        functools.partial(_attn_sample_kernel, pp=pp, t_new=t_new, n_pages=n_pages),
        grid_spec=grid_spec,
        out_shape=jax.ShapeDtypeStruct((nb, t_new, width), F32),
        compiler_params=_cparams(("arbitrary",)),
        name="attn_sample",
    )(page_table, qbd, bias_cols, k_new, v_new, cache_k, cache_v)


CONV_PAD = 32


def _conv_kernel(ext_ref, w_ref, b_ref, g_ref, beta_ref, o_ref, y_ref, *, seq, tt, lc):
    width = o_ref.shape[-1]
    lead = CONV_PAD - CONV_STATE

    def chunk(t0):
        for c in range(width // lc):
            cs = slice(c * lc, (c + 1) * lc)
            win = ext_ref[pl.ds(t0, tt + CONV_PAD), cs]
            acc = jnp.zeros((tt, lc), F32)
            for i in range(CONV_KERNEL):
                acc = acc + w_ref[i:i + 1, cs] * win[i + lead:i + lead + tt, :]
            y_ref[:, cs] = acc + b_ref[:, cs]
        y = y_ref[...]
        mu = jnp.mean(y, axis=-1, keepdims=True)
        d = y - mu
        var = jnp.mean(d * d, axis=-1, keepdims=True)
        yn = d * lax.rsqrt(var + EPS) * g_ref[...] + beta_ref[...]
        o_ref[pl.ds(t0, tt), :] = (yn * _sigmoid(yn)).astype(o_ref.dtype)

    if seq == tt:
        chunk(0)
    else:
        def body(n, _):
            chunk(pl.multiple_of(n * tt, tt))
            return 0
        lax.fori_loop(0, seq // tt, body, 0)


def _conv_branch(ext, conv_w, conv_b, ln_g, ln_b, tt):
    nseq, rows, width = ext.shape
    seq = rows - CONV_PAD
    vec = pl.BlockSpec((1, width), lambda n: (0, 0))
    return pl.pallas_call(
        functools.partial(_conv_kernel, seq=seq, tt=tt, lc=LANE),
        grid=(nseq,),
        in_specs=[pl.BlockSpec((None, rows, width), lambda n: (n, 0, 0)),
                  pl.BlockSpec((CONV_KERNEL, width), lambda n: (0, 0)), vec, vec, vec],
        out_specs=pl.BlockSpec((None, seq, width), lambda n: (n, 0, 0)),
        out_shape=jax.ShapeDtypeStruct((nseq, seq, width), BF16),
        scratch_shapes=[pltpu.VMEM((tt, width), F32)],
        compiler_params=_cparams(("parallel",)),
        name="conv_branch",
    )(ext, conv_w, conv_b.reshape(1, width), ln_g.reshape(1, width), ln_b.reshape(1, width))


def _norm_cast_pair(xs, g):
    return tuple(_norm_cast(x, g, 256 if x.shape[0] % 256 == 0 else x.shape[0]) for x in xs)


def _project(xs, ln1_g, q_g, k_g, w_in, tm, tn):
    d = xs[0].shape[1]
    aw = N_HEADS * HEAD_DIM
    cw = d // 2
    xn = _norm_cast_pair(xs, ln1_g)
    gain = lambda g: [(g.reshape(1, HEAD_DIM), (1, HEAD_DIM), lambda j, i: (0, 0))]
    (q,) = _mm("proj_q", [xn], [w_in], [0], [0], gain(q_g * HEAD_DIM ** -0.5), [], [(aw, BF16)], _ep_headnorm, tm, tn)
    k32, k16 = _mm("proj_k", [xn], [w_in], [0], [aw], gain(k_g), [], [(aw, F32), (aw, BF16)], _ep_headnorm, tm, tn)
    v32, v16 = _mm("proj_v", [xn], [w_in], [0], [2 * aw], [], [], [(aw, F32), (aw, BF16)], _ep_copy, tm, tn)
    (u,) = _mm("proj_u", [xn], [w_in, w_in], [0, 0], [3 * aw, 3 * aw + cw], [], [], [(cw, F32)], _ep_glu, tm, tn // 2)
    (gates,) = _mm("proj_gates", [xn], [w_in], [0], [3 * aw + 2 * cw], [], [], [(2 * d, BF16)], _ep_sigmoid, tm, tn)
    return q, k32, k16, v32, v16, u, gates


def _merge_mlp(xs, attn, c, gates, w_attn_out, w_conv_out, w_out, ln2_g, w_up, w_down_b, tm, tn):
    d = xs[0].shape[1]
    (mrg,) = _mm("merge", [attn, c], [w_attn_out, w_conv_out], [0, 1], [0, 0], [], [(gates, 0), (gates, d)],
                 [(d, BF16)], _ep_merge, tm, tn)
    (h,) = _mm("out_proj", [mrg], [w_out], [0], [0], [], [(xs, 0)], [(d, F32)], _ep_residual, tm, tn)
    hn = _norm_cast_pair(h, ln2_g)
    (f,) = _mm("mlp_up", [hn], [w_up], [0], [0], [], [], [(w_up.shape[1], BF16)], _ep_relu2, tm, tn)
    return tuple(_mm_k_residual("mlp_down", fi, w_down_b, hi, min(tm, fi.shape[0]), 2 * tn, 2048)
                 for fi, hi in zip(f, h))


def kernel(x_prompt, x_sample, cache_k, cache_v, state_conv, page_table, meta_tokens, ln1_g, q_norm_g, k_norm_g,
           sb_bias, w_in, conv_w, conv_b, conv_ln_g, conv_ln_b, w_attn_out, w_conv_out, w_out, ln2_g, w_up, w_down):
    assert w_in.shape[0] == 1, "single-layer problem"
    batch, seq, d = x_prompt.shape
    nb, t_new, _ = x_sample.shape
    aw = N_HEADS * HEAD_DIM
    cw = d // 2
    page = cache_k.shape[2]
    w_down_b = w_down[0].astype(BF16)
    ln1, qg, kg, bias = ln1_g[0], q_norm_g[0], k_norm_g[0], sb_bias[0]
    cv_w, cv_b, cv_g, cv_beta = conv_w[0], conv_b[0], conv_ln_g[0], conv_ln_b[0]

    xp = x_prompt.reshape(batch * seq, d)
    xs = jnp.concatenate([meta_tokens, x_sample.reshape(nb * t_new, d)], axis=0)
    tm, tn = 1024, 512
    ((qp, qs), (kp32, ks32), (kp16, ks16), (vp32, vs32), (vp16, vs16), (up, us), gates) = _project(
        (xp, xs), ln1, qg, kg, w_in[0], tm, tn)

    pad_meta = lambda a: jnp.pad(a[:N_META], ((0, LANE - N_META), (0, 0)))
    qm, km, vm = pad_meta(qs), pad_meta(ks16), pad_meta(vs16)
    attn_p = _attn_prompt(qp, kp16, vp16, km, vm, bias, batch, seq, tq=256, nh=4)
    attn_m = _attn_meta(qm, km, vm, bias)[:N_META]

    q4 = qs[N_META:].reshape(nb, t_new, N_HEADS, HEAD_DIM)
    qbd = jnp.einsum("bthd,hg->bhdgt", q4, jnp.eye(N_HEADS, dtype=BF16)).reshape(nb, aw, N_HEADS * t_new)
    bias_cols = jnp.repeat(bias.astype(F32), t_new).reshape(1, N_HEADS * t_new)
    pad_new = lambda a: jnp.pad(a[N_META:].reshape(nb, t_new, aw), ((0, 0), (0, page - t_new), (0, 0)))
    attn_s = _attn_sample(page_table, qbd, bias_cols, pad_new(ks16), pad_new(vs16),
                          cache_k[0], cache_v[0], t_new, 4)
    attn_small = jnp.concatenate([attn_m, attn_s.reshape(nb * t_new, aw).astype(BF16)], axis=0)

    lead = CONV_PAD - CONV_STATE
    u_meta = us[:N_META]
    u_s = us[N_META:].reshape(nb, t_new, cw)
    u_p = up.reshape(batch, seq, cw)
    hist_p = jnp.concatenate([jnp.zeros((CONV_PAD - N_META, cw), F32), u_meta], axis=0)
    ext_p = jnp.concatenate([jnp.broadcast_to(hist_p[None], (batch, CONV_PAD, cw)), u_p], axis=1)
    ext_m = jnp.concatenate([jnp.zeros((CONV_PAD, cw), F32), u_meta], axis=0)[None]
    ext_s = jnp.concatenate([jnp.zeros((nb, lead, cw), F32), state_conv[0], u_s], axis=1)
    c_p = _conv_branch(ext_p, cv_w, cv_b, cv_g, cv_beta, 64).reshape(batch * seq, cw)
    c_m = _conv_branch(ext_m, cv_w, cv_b, cv_g, cv_beta, N_META).reshape(N_META, cw)
    c_s = _conv_branch(ext_s, cv_w, cv_b, cv_g, cv_beta, t_new).reshape(nb * t_new, cw)
    c_small = jnp.concatenate([c_m, c_s], axis=0)

    y_p, y_s = _merge_mlp((xp, xs), (attn_p, attn_small), (c_p, c_small), gates, w_attn_out[0], w_conv_out[0],
                          w_out[0], ln2_g[0], w_up[0], w_down_b, tm, tn)

    def with_meta(real, small):
        meta = jnp.broadcast_to(small[:N_META].reshape(1, N_META, N_HEADS, HEAD_DIM),
                                (batch, N_META, N_HEADS, HEAD_DIM))
        return jnp.concatenate([meta, real.reshape(batch, seq, N_HEADS, HEAD_DIM)], axis=1)[None]

    heads = lambda a: a[N_META:].reshape(1, nb, t_new, N_HEADS, HEAD_DIM)
    return (y_p.reshape(batch, seq, d),
            y_s[N_META:].reshape(nb, t_new, d),
            with_meta(kp32, ks32),
            with_meta(vp32, vs32),
            ext_p[:, -CONV_STATE:][None],
            heads(ks32),
            heads(vs32),
            ext_s[:, -CONV_STATE:][None])
```

```python
import functools

import jax
import jax.numpy as jnp
from jax import lax
from jax.experimental import pallas as pl
from jax.experimental.pallas import tpu as pltpu

N_HEADS = 16
HEAD_DIM = 128
N_META = 16
CONV_KERNEL = 31
CONV_STATE = CONV_KERNEL - 1
EPS = 1e-6
LANE = 128
VMEM_LIMIT = 56 * 1024 * 1024
F32 = jnp.float32
BF16 = jnp.bfloat16


def _cparams(sem):
    return pltpu.CompilerParams(dimension_semantics=sem, vmem_limit_bytes=VMEM_LIMIT)


def _norm_cast_kernel(x_ref, g_ref, o_ref):
    x = x_ref[...]
    ms = jnp.mean(x * x, axis=-1, keepdims=True)
    o_ref[...] = (x * lax.rsqrt(ms + EPS) * g_ref[...]).astype(o_ref.dtype)


def _norm_cast(x, g, tm):
    m, d = x.shape
    return pl.pallas_call(
        _norm_cast_kernel,
        grid=(m // tm,),
        in_specs=[pl.BlockSpec((tm, d), lambda i: (i, 0)),
                  pl.BlockSpec((1, d), lambda i: (0, 0))],
        out_specs=pl.BlockSpec((tm, d), lambda i: (i, 0)),
        out_shape=jax.ShapeDtypeStruct((m, d), BF16),
        compiler_params=_cparams(("parallel",)),
        name="norm_cast",
    )(x, g.reshape(1, d))


def _mm_kernel(*refs, nx, wx, nconst, nextra, nout, epilogue):
    nw = len(wx)
    pos = 0

    def take(n):
        nonlocal pos
        pos += n
        return refs[pos - n:pos]

    x_big, x_small, w_refs, consts = take(nx), take(nx), take(nw), take(nconst)
    e_big, e_small, o_big, o_small = take(nextra), take(nextra), take(nout), take(nout)
    i = pl.program_id(1)

    def run(x_refs, e_refs, o_refs):
        accs = [jnp.dot(x_refs[xi][...], w[...].astype(BF16), preferred_element_type=F32)
                for xi, w in zip(wx, w_refs)]
        epilogue(accs, consts + e_refs, o_refs)

    @pl.when(i == 0)
    def _():
        run(x_small, e_small, o_small)

    @pl.when(i > 0)
    def _():
        run(x_big, e_big, o_big)


def _mm(name, xs, ws, wx, w_col_off, consts, extras, outs, epilogue, tm, tn):
    mb, ms = xs[0][0].shape[0], xs[0][1].shape[0]
    n_big = mb // tm
    n_cols = outs[0][0]

    def big_row(j, i):
        r = jnp.maximum(i - 1, 0)
        return jnp.where(j % 2 == 0, r, n_big - 1 - r)

    in_specs = [pl.BlockSpec((tm, xb.shape[1]), lambda j, i: (big_row(j, i), 0)) for xb, _ in xs]
    in_specs += [pl.BlockSpec((ms, xm.shape[1]), lambda j, i: (0, 0)) for _, xm in xs]
    in_specs += [pl.BlockSpec((w.shape[0], tn), lambda j, i, blk=off // tn: (0, j + blk)) for w, off in zip(ws, w_col_off)]
    in_specs += [pl.BlockSpec(bs, im) for _, bs, im in consts]
    in_specs += [pl.BlockSpec((tm, tn), lambda j, i, blk=off // tn: (big_row(j, i), j + blk)) for _, off in extras]
    in_specs += [pl.BlockSpec((ms, tn), lambda j, i, blk=off // tn: (0, j + blk)) for _, off in extras]
    out_specs = [pl.BlockSpec((tm, tn), lambda j, i: (big_row(j, i), j)) for _ in outs]
    out_specs += [pl.BlockSpec((ms, tn), lambda j, i: (0, j)) for _ in outs]
    res = pl.pallas_call(
        functools.partial(_mm_kernel, nx=len(xs), wx=tuple(wx), nconst=len(consts), nextra=len(extras),
                          nout=len(outs), epilogue=epilogue),
        grid=(n_cols // tn, n_big + 1),
        in_specs=in_specs,
        out_specs=out_specs,
        out_shape=[jax.ShapeDtypeStruct((mb, nc), dt) for nc, dt in outs]
                  + [jax.ShapeDtypeStruct((ms, nc), dt) for nc, dt in outs],
        compiler_params=_cparams(("parallel", "arbitrary")),
        name=name,
    )(*[xb for xb, _ in xs], *[xm for _, xm in xs], *ws, *[c for c, _, _ in consts],
      *[e[0] for e, _ in extras], *[e[1] for e, _ in extras])
    return [(res[k], res[len(outs) + k]) for k in range(len(outs))]


def _sigmoid(x):
    return 1.0 / (1.0 + jnp.exp(-x))


def _ep_headnorm(accs, e_refs, o_refs):
    (acc,), (g_ref,) = accs, e_refs
    g = g_ref[...]
    for hh in range(acc.shape[1] // HEAD_DIM):
        sl = slice(hh * HEAD_DIM, (hh + 1) * HEAD_DIM)
        blk = acc[:, sl]
        y = blk * lax.rsqrt(jnp.mean(blk * blk, axis=-1, keepdims=True) + EPS) * g
        for o in o_refs:
            o[:, sl] = y.astype(o.dtype)


def _ep_copy(accs, e_refs, o_refs):
    for o in o_refs:
        o[...] = accs[0].astype(o.dtype)


def _ep_glu(accs, e_refs, o_refs):
    o_refs[0][...] = accs[0] * _sigmoid(accs[1])


def _ep_sigmoid(accs, e_refs, o_refs):
    o_refs[0][...] = _sigmoid(accs[0]).astype(o_refs[0].dtype)


def _ep_merge(accs, e_refs, o_refs):
    ga = e_refs[0][...].astype(F32)
    gb = e_refs[1][...].astype(F32)
    o_refs[0][...] = (ga * accs[0] + gb * accs[1]).astype(o_refs[0].dtype)


def _ep_residual(accs, e_refs, o_refs):
    o_refs[0][...] = e_refs[0][...] + accs[0]


def _ep_relu2(accs, e_refs, o_refs):
    r = jnp.maximum(accs[0], 0.0)
    o_refs[0][...] = (r * r).astype(o_refs[0].dtype)


def _mm_k_kernel(x_ref, w_ref, r_ref, o_ref, acc_ref):
    k = pl.program_id(2)

    @pl.when(k == 0)
    def _():
        acc_ref[...] = jnp.zeros_like(acc_ref)

    acc_ref[...] += jnp.dot(x_ref[...], w_ref[...], preferred_element_type=F32)

    @pl.when(k == pl.num_programs(2) - 1)
    def _():
        o_ref[...] = r_ref[...] + acc_ref[...]


def _mm_k_residual(name, x, w, r, tm, tn, tk):
    m, kk = x.shape
    n = w.shape[1]
    return pl.pallas_call(
        _mm_k_kernel,
        grid=(m // tm, n // tn, kk // tk),
        in_specs=[pl.BlockSpec((tm, tk), lambda i, j, k: (i, k)),
                  pl.BlockSpec((tk, tn), lambda i, j, k: (k, j)),
                  pl.BlockSpec((tm, tn), lambda i, j, k: (i, j))],
        out_specs=pl.BlockSpec((tm, tn), lambda i, j, k: (i, j)),
        out_shape=jax.ShapeDtypeStruct((m, n), F32),
        scratch_shapes=[pltpu.VMEM((tm, tn), F32)],
        compiler_params=_cparams(("parallel", "parallel", "arbitrary")),
        name=name,
    )(x, w, r)


def _split_bf16(x):
    hi = x.astype(BF16)
    lo = (x - hi.astype(F32)).astype(BF16)
    return hi, lo


def _neg_log_keep(z, vis):
    sp = jnp.maximum(z, 0.0) + jnp.log(1.0 + jnp.exp(-jnp.abs(z)))
    return sp if vis is None else jnp.where(vis, sp, 0.0)


def _strict_tri_neg(n, newer_on_rows):
    r = lax.broadcasted_iota(jnp.int32, (n, n), 0)
    c = lax.broadcasted_iota(jnp.int32, (n, n), 1)
    return jnp.where((r > c) if newer_on_rows else (c > r), -1.0, 0.0).astype(BF16)


def _sb_tile(q, k, v, bias, vis, u_neg, acc, carry):
    tq = q.shape[0]
    z = lax.dot_general(q, k, (((1,), (1,)), ((), ())), preferred_element_type=F32) + bias
    sp = _neg_log_keep(z, vis)
    hi, lo = _split_bf16(sp)
    r = jnp.dot(jnp.concatenate([hi, lo], axis=0), u_neg, preferred_element_type=F32)
    w = jnp.exp(z - sp + (r[:tq] + r[tq:]) + carry)
    if vis is not None:
        w = jnp.where(vis, w, 0.0)
    acc = acc + jnp.dot(w.astype(BF16), v, preferred_element_type=F32)
    return acc, carry - jnp.sum(sp, axis=1, keepdims=True)


def _attn_prompt_kernel(bias_ref, q_ref, k_ref, v_ref, km_ref, vm_ref, u_ref, o_ref,
                        sp_ref, ls_ref, w_ref, acc_ref, carry_ref, tot_ref, *, tq, tk, nh):
    hg = pl.program_id(1)
    i = pl.program_id(2)
    tmeta = km_ref.shape[0]
    row = lax.broadcasted_iota(jnp.int32, (tq, tk), 0)
    col = lax.broadcasted_iota(jnp.int32, (tq, tk), 1)
    vis_diag = col < row
    vis_meta = lax.broadcasted_iota(jnp.int32, (tq, tmeta), 1) < N_META
    heads = [slice(hh * HEAD_DIM, (hh + 1) * HEAD_DIM) for hh in range(nh)]
    acc_ref[...] = jnp.zeros_like(acc_ref)
    carry_ref[...] = jnp.zeros_like(carry_ref)

    def tile_group(load_k, load_v, vis, u, width):
        cols = slice(0, width)
        for hh, sl in enumerate(heads):
            z = lax.dot_general(q_ref[:, sl], load_k(sl), (((1,), (1,)), ((), ())),
                                preferred_element_type=F32) + bias_ref[hg * nh + hh]
            sp = _neg_log_keep(z, vis)
            hi, lo = _split_bf16(sp)
            sp_ref[hh, :tq, cols] = hi
            sp_ref[hh, tq:, cols] = lo
            ls_ref[hh, :, cols] = z - sp
            tot_ref[hh] = jnp.broadcast_to(jnp.sum(sp, axis=1, keepdims=True), (tq, HEAD_DIM))
        for hh, sl in enumerate(heads):
            r = jnp.dot(sp_ref[hh, :, cols], u, preferred_element_type=F32)
            c = carry_ref[hh]
            c = c if width == HEAD_DIM else jnp.concatenate([c] * (width // HEAD_DIM), axis=1)
            w = jnp.exp(ls_ref[hh, :, cols] + (r[:tq] + r[tq:]) + c)
            if vis is not None:
                w = jnp.where(vis, w, 0.0)
            w_ref[hh, :, cols] = w.astype(BF16)
        for hh, sl in enumerate(heads):
            acc_ref[hh] += jnp.dot(w_ref[hh, :, cols], load_v(sl), preferred_element_type=F32)
            carry_ref[hh] -= tot_ref[hh]

    def chunk(j, vis):
        off = pl.multiple_of(j * tk, tk)
        tile_group(lambda sl: k_ref[pl.ds(off, tk), sl], lambda sl: v_ref[pl.ds(off, tk), sl], vis, u_ref[...], tk)

    chunk(i, vis_diag)

    def body(s, _):
        chunk(i - 1 - s, None)
        return 0

    lax.fori_loop(0, i, body, 0)
    tile_group(lambda sl: km_ref[:, sl], lambda sl: vm_ref[:, sl], vis_meta, u_ref[:tmeta, :tmeta], tmeta)
    for hh, sl in enumerate(heads):
        o_ref[:, sl] = acc_ref[hh].astype(o_ref.dtype)


def _attn_prompt(q, k, v, km, vm, sb_bias, batch, seq, tq, nh):
    tk = tq
    nq = seq // tq
    wh = nh * HEAD_DIM
    qo_spec = pl.BlockSpec((tq, wh), lambda b, h, i: (b * nq + i, h))
    kv_spec = pl.BlockSpec((seq, wh), lambda b, h, i: (b, h))
    m_spec = pl.BlockSpec((km.shape[0], wh), lambda b, h, i: (0, h))
    return pl.pallas_call(
        functools.partial(_attn_prompt_kernel, tq=tq, tk=tk, nh=nh),
        grid=(batch, N_HEADS // nh, nq),
        in_specs=[pl.BlockSpec(memory_space=pltpu.SMEM), qo_spec, kv_spec, kv_spec, m_spec, m_spec,
                  pl.BlockSpec((tk, tk), lambda b, h, i: (0, 0))],
        out_specs=qo_spec,
        out_shape=jax.ShapeDtypeStruct(q.shape, BF16),
        scratch_shapes=[pltpu.VMEM((nh, 2 * tq, tk), BF16), pltpu.VMEM((nh, tq, tk), F32),
                        pltpu.VMEM((nh, tq, tk), BF16), pltpu.VMEM((nh, tq, HEAD_DIM), F32),
                        pltpu.VMEM((nh, tq, HEAD_DIM), F32), pltpu.VMEM((nh, tq, HEAD_DIM), F32)],
        compiler_params=_cparams(("parallel", "parallel", "arbitrary")),
        name="attn_prompt",
    )(sb_bias, q, k, v, km, vm, _strict_tri_neg(tk, True))


def _attn_meta_kernel(bias_ref, q_ref, k_ref, v_ref, o_ref):
    tq = q_ref.shape[0]
    row = lax.broadcasted_iota(jnp.int32, (tq, tq), 0)
    col = lax.broadcasted_iota(jnp.int32, (tq, tq), 1)
    acc, _ = _sb_tile(q_ref[...], k_ref[...], v_ref[...], bias_ref[pl.program_id(0)],
                      (col < row) & (col < N_META), _strict_tri_neg(tq, True),
                      jnp.zeros((tq, HEAD_DIM), F32), jnp.zeros((tq, 1), F32))
    o_ref[...] = acc.astype(o_ref.dtype)


def _attn_meta(q, k, v, sb_bias):
    spec = pl.BlockSpec((q.shape[0], HEAD_DIM), lambda h: (0, h))
    return pl.pallas_call(
        _attn_meta_kernel,
        grid=(N_HEADS,),
        in_specs=[pl.BlockSpec(memory_space=pltpu.SMEM), spec, spec, spec],
        out_specs=spec,
        out_shape=jax.ShapeDtypeStruct(q.shape, BF16),
        compiler_params=_cparams(("parallel",)),
        name="attn_meta",
    )(sb_bias, q, k, v)


def _sb_page_weights(z, vis, ut_neg, carry):
    nc = z.shape[1]
    sp = _neg_log_keep(z, vis)
    hi, lo = _split_bf16(sp)
    r = jnp.dot(ut_neg, jnp.concatenate([hi, lo], axis=1), preferred_element_type=F32)
    w = jnp.exp(z - sp + (r[:, :nc] + r[:, nc:]) + carry)
    if vis is not None:
        w = jnp.where(vis, w, 0.0)
    return w, carry - jnp.sum(sp, axis=0, keepdims=True)


HEAD_PITCH = 24


def _attn_sample_kernel(pt_ref, qbd_ref, bias_ref, kn_ref, vn_ref, ck_ref, cv_ref, o_ref,
                        kbuf, vbuf, sem, acc_ref, *, pp, t_new, n_pages):
    b = pl.program_id(0)
    page = kn_ref.shape[0]
    nc = qbd_ref.shape[1]
    ngroups = n_pages // pp
    qbd = qbd_ref[...]
    bias_cols = bias_ref[...]
    ut_neg = _strict_tri_neg(page, False)

    def copies(bb, g, slot):
        out = []
        for p in range(pp):
            idx = pt_ref[bb, n_pages - 1 - (g * pp + p)]
            for t, (src, dst) in enumerate(((ck_ref, kbuf), (cv_ref, vbuf))):
                out.append(pltpu.make_async_copy(src.at[idx], dst.at[slot, p, :, pl.ds(0, N_HEADS), :],
                                                 sem.at[t, slot, p]))
        return out

    def scores(k):
        return jnp.dot(k, qbd, preferred_element_type=F32) + bias_cols

    def accumulate(w, v):
        acc_ref[...] += jnp.dot(w.T.astype(BF16), v, preferred_element_type=F32)

    def heads_major(buf, slot):
        return jnp.concatenate(
            [jnp.concatenate([buf.at[slot, p].reshape(page * HEAD_PITCH, HEAD_DIM)[pl.ds(hh, page, stride=HEAD_PITCH), :]
                              for hh in range(N_HEADS)], axis=1) for p in range(pp)],
            axis=0).astype(BF16)

    @pl.when(b == 0)
    def _():
        for c in copies(0, 0, 0):
            c.start()

    acc_ref[...] = jnp.zeros_like(acc_ref)
    key = lax.broadcasted_iota(jnp.int32, (page, nc), 0)
    t = lax.broadcasted_iota(jnp.int32, (page, nc), 1) % t_new
    w, carry = _sb_page_weights(scores(kn_ref[...]), key < t, ut_neg, jnp.zeros((1, nc), F32))
    accumulate(w, vn_ref[...])

    def body(g, carry):
        slot = g % 2

        @pl.when(g + 1 < ngroups)
        def _():
            for c in copies(b, g + 1, 1 - slot):
                c.start()

        @pl.when((g + 1 == ngroups) & (b + 1 < pl.num_programs(0)))
        def _():
            for c in copies(b + 1, 0, 0):
                c.start()

        for c in copies(b, g, slot):
            c.wait()
        z = scores(heads_major(kbuf, slot))
        ws = []
        for p in range(pp):
            w, carry = _sb_page_weights(z[p * page:(p + 1) * page], None, ut_neg, carry)
            ws.append(w)
        accumulate(jnp.concatenate(ws, axis=0), heads_major(vbuf, slot))
        return carry

    lax.fori_loop(0, ngroups, body, carry)
    for hh in range(N_HEADS):
        sl = slice(hh * HEAD_DIM, (hh + 1) * HEAD_DIM)
        o_ref[:, sl] = acc_ref[hh * t_new:(hh + 1) * t_new, sl]


def _attn_sample(page_table, qbd, bias_cols, k_new, v_new, cache_k, cache_v, t_new, pp):
    nb, n_pages = page_table.shape
    assert (n_pages // pp) % 2 == 0, "page groups alternate between two buffer slots, starting each sequence in slot 0"
    page, width = k_new.shape[1:]
    nc = qbd.shape[2]
    per_b = lambda b, pt: (b, 0, 0)
    grid_spec = pltpu.PrefetchScalarGridSpec(
        num_scalar_prefetch=1,
        grid=(nb,),
        in_specs=[pl.BlockSpec((None, width, nc), per_b),
                  pl.BlockSpec((1, nc), lambda b, pt: (0, 0)),
                  pl.BlockSpec((None, page, width), per_b),
                  pl.BlockSpec((None, page, width), per_b),
                  pl.BlockSpec(memory_space=pl.ANY),
                  pl.BlockSpec(memory_space=pl.ANY)],
        out_specs=pl.BlockSpec((None, t_new, width), per_b),
        scratch_shapes=[pltpu.VMEM((2, pp, page, HEAD_PITCH, HEAD_DIM), F32),
                        pltpu.VMEM((2, pp, page, HEAD_PITCH, HEAD_DIM), F32),
                        pltpu.SemaphoreType.DMA((2, 2, pp)),
                        pltpu.VMEM((nc, width), F32)],
    )
    return pl.pallas_call(
        functools.partial(_attn_sample_kernel, pp=pp, t_new=t_new, n_pages=n_pages),
        grid_spec=grid_spec,
        out_shape=jax.ShapeDtypeStruct((nb, t_new, width), F32),
        compiler_params=_cparams(("arbitrary",)),
        name="attn_sample",
    )(page_table, qbd, bias_cols, k_new, v_new, cache_k, cache_v)


CONV_PAD = 32


def _conv_kernel(ext_ref, w_ref, b_ref, g_ref, beta_ref, o_ref, y_ref, *, seq, tt, lc):
    width = o_ref.shape[-1]
    lead = CONV_PAD - CONV_STATE

    def chunk(t0):
        for c in range(width // lc):
            cs = slice(c * lc, (c + 1) * lc)
            win = ext_ref[pl.ds(t0, tt + CONV_PAD), cs]
            acc = jnp.zeros((tt, lc), F32)
            for i in range(CONV_KERNEL):
                acc = acc + w_ref[i:i + 1, cs] * win[i + lead:i + lead + tt, :]
            y_ref[:, cs] = acc + b_ref[:, cs]
        y = y_ref[...]
        mu = jnp.mean(y, axis=-1, keepdims=True)
        d = y - mu
        var = jnp.mean(d * d, axis=-1, keepdims=True)
        yn = d * lax.rsqrt(var + EPS) * g_ref[...] + beta_ref[...]
        o_ref[pl.ds(t0, tt), :] = (yn * _sigmoid(yn)).astype(o_ref.dtype)

    if seq == tt:
        chunk(0)
    else:
        def body(n, _):
            chunk(pl.multiple_of(n * tt, tt))
            return 0
        lax.fori_loop(0, seq // tt, body, 0)


def _conv_branch(ext, conv_w, conv_b, ln_g, ln_b, tt):
    nseq, rows, width = ext.shape
    seq = rows - CONV_PAD
    vec = pl.BlockSpec((1, width), lambda n: (0, 0))
    return pl.pallas_call(
        functools.partial(_conv_kernel, seq=seq, tt=tt, lc=LANE),
        grid=(nseq,),
        in_specs=[pl.BlockSpec((None, rows, width), lambda n: (n, 0, 0)),
                  pl.BlockSpec((CONV_KERNEL, width), lambda n: (0, 0)), vec, vec, vec],
        out_specs=pl.BlockSpec((None, seq, width), lambda n: (n, 0, 0)),
        out_shape=jax.ShapeDtypeStruct((nseq, seq, width), BF16),
        scratch_shapes=[pltpu.VMEM((tt, width), F32)],
        compiler_params=_cparams(("parallel",)),
        name="conv_branch",
    )(ext, conv_w, conv_b.reshape(1, width), ln_g.reshape(1, width), ln_b.reshape(1, width))


def _norm_cast_pair(xs, g):
    return tuple(_norm_cast(x, g, 256 if x.shape[0] % 256 == 0 else x.shape[0]) for x in xs)


def _project(xs, ln1_g, q_g, k_g, w_in, tm, tn):
    d = xs[0].shape[1]
    aw = N_HEADS * HEAD_DIM
    cw = d // 2
    xn = _norm_cast_pair(xs, ln1_g)
    gain = lambda g: [(g.reshape(1, HEAD_DIM), (1, HEAD_DIM), lambda j, i: (0, 0))]
    (q,) = _mm("proj_q", [xn], [w_in], [0], [0], gain(q_g * HEAD_DIM ** -0.5), [], [(aw, BF16)], _ep_headnorm, tm, tn)
    k32, k16 = _mm("proj_k", [xn], [w_in], [0], [aw], gain(k_g), [], [(aw, F32), (aw, BF16)], _ep_headnorm, tm, tn)
    v32, v16 = _mm("proj_v", [xn], [w_in], [0], [2 * aw], [], [], [(aw, F32), (aw, BF16)], _ep_copy, tm, tn)
    (u,) = _mm("proj_u", [xn], [w_in, w_in], [0, 0], [3 * aw, 3 * aw + cw], [], [], [(cw, F32)], _ep_glu, tm, tn // 2)
    (gates,) = _mm("proj_gates", [xn], [w_in], [0], [3 * aw + 2 * cw], [], [], [(2 * d, BF16)], _ep_sigmoid, tm, tn)
    return q, k32, k16, v32, v16, u, gates


def _merge_mlp(xs, attn, c, gates, w_attn_out, w_conv_out, w_out, ln2_g, w_up, w_down_b, tm, tn):
    d = xs[0].shape[1]
    (mrg,) = _mm("merge", [attn, c], [w_attn_out, w_conv_out], [0, 1], [0, 0], [], [(gates, 0), (gates, d)],
                 [(d, BF16)], _ep_merge, tm, tn)
    (h,) = _mm("out_proj", [mrg], [w_out], [0], [0], [], [(xs, 0)], [(d, F32)], _ep_residual, tm, tn)
    hn = _norm_cast_pair(h, ln2_g)
    (f,) = _mm("mlp_up", [hn], [w_up], [0], [0], [], [], [(w_up.shape[1], BF16)], _ep_relu2, tm, tn)
    return tuple(_mm_k_residual("mlp_down", fi, w_down_b, hi, min(tm, fi.shape[0]), 2 * tn, 2048)
                 for fi, hi in zip(f, h))


def kernel(x_prompt, x_sample, cache_k, cache_v, state_conv, page_table, meta_tokens, ln1_g, q_norm_g, k_norm_g,
           sb_bias, w_in, conv_w, conv_b, conv_ln_g, conv_ln_b, w_attn_out, w_conv_out, w_out, ln2_g, w_up, w_down):
    assert w_in.shape[0] == 1, "single-layer problem"
    batch, seq, d = x_prompt.shape
    nb, t_new, _ = x_sample.shape
    aw = N_HEADS * HEAD_DIM
    cw = d // 2
    page = cache_k.shape[2]
    w_down_b = w_down[0].astype(BF16)
    ln1, qg, kg, bias = ln1_g[0], q_norm_g[0], k_norm_g[0], sb_bias[0]
    cv_w, cv_b, cv_g, cv_beta = conv_w[0], conv_b[0], conv_ln_g[0], conv_ln_b[0]

    xp = x_prompt.reshape(batch * seq, d)
    xs = jnp.concatenate([meta_tokens, x_sample.reshape(nb * t_new, d)], axis=0)
    tm, tn = 1024, 512
    ((qp, qs), (kp32, ks32), (kp16, ks16), (vp32, vs32), (vp16, vs16), (up, us), gates) = _project(
        (xp, xs), ln1, qg, kg, w_in[0], tm, tn)

    pad_meta = lambda a: jnp.pad(a[:N_META], ((0, LANE - N_META), (0, 0)))
    qm, km, vm = pad_meta(qs), pad_meta(ks16), pad_meta(vs16)
    attn_p = _attn_prompt(qp, kp16, vp16, km, vm, bias, batch, seq, tq=256, nh=4)
    attn_m = _attn_meta(qm, km, vm, bias)[:N_META]

    q4 = qs[N_META:].reshape(nb, t_new, N_HEADS, HEAD_DIM)
    qbd = jnp.einsum("bthd,hg->bhdgt", q4, jnp.eye(N_HEADS, dtype=BF16)).reshape(nb, aw, N_HEADS * t_new)
    bias_cols = jnp.repeat(bias.astype(F32), t_new).reshape(1, N_HEADS * t_new)
    pad_new = lambda a: jnp.pad(a[N_META:].reshape(nb, t_new, aw), ((0, 0), (0, page - t_new), (0, 0)))
    attn_s = _attn_sample(page_table, qbd, bias_cols, pad_new(ks16), pad_new(vs16),
                          cache_k[0], cache_v[0], t_new, 4)
    attn_small = jnp.concatenate([attn_m, attn_s.reshape(nb * t_new, aw).astype(BF16)], axis=0)

    lead = CONV_PAD - CONV_STATE
    u_meta = us[:N_META]
    u_s = us[N_META:].reshape(nb, t_new, cw)
    u_p = up.reshape(batch, seq, cw)
    hist_p = jnp.concatenate([jnp.zeros((CONV_PAD - N_META, cw), F32), u_meta], axis=0)
    ext_p = jnp.concatenate([jnp.broadcast_to(hist_p[None], (batch, CONV_PAD, cw)), u_p], axis=1)
    ext_m = jnp.concatenate([jnp.zeros((CONV_PAD, cw), F32), u_meta], axis=0)[None]
    ext_s = jnp.concatenate([jnp.zeros((nb, lead, cw), F32), state_conv[0], u_s], axis=1)
    c_p = _conv_branch(ext_p, cv_w, cv_b, cv_g, cv_beta, 64).reshape(batch * seq, cw)
    c_m = _conv_branch(ext_m, cv_w, cv_b, cv_g, cv_beta, N_META).reshape(N_META, cw)
    c_s = _conv_branch(ext_s, cv_w, cv_b, cv_g, cv_beta, t_new).reshape(nb * t_new, cw)
    c_small = jnp.concatenate([c_m, c_s], axis=0)

    y_p, y_s = _merge_mlp((xp, xs), (attn_p, attn_small), (c_p, c_small), gates, w_attn_out[0], w_conv_out[0],
                          w_out[0], ln2_g[0], w_up[0], w_down_b, tm, tn)

    def with_meta(real, small):
        meta = jnp.broadcast_to(small[:N_META].reshape(1, N_META, N_HEADS, HEAD_DIM),
                                (batch, N_META, N_HEADS, HEAD_DIM))
        return jnp.concatenate([meta, real.reshape(batch, seq, N_HEADS, HEAD_DIM)], axis=1)[None]

    heads = lambda a: a[N_META:].reshape(1, nb, t_new, N_HEADS, HEAD_DIM)
    return (y_p.reshape(batch, seq, d),
            y_s[N_META:].reshape(nb, t_new, d),
            with_meta(kp32, ks32),
            with_meta(vp32, vs32),
            ext_p[:, -CONV_STATE:][None],
            heads(ks32),
            heads(vs32),
            ext_s[:, -CONV_STATE:][None])
```
